```python
import math
import jax, jax.numpy as jnp
from jax import lax
import numpy as np

D_MODEL = 1024
BATCH = 4
SEQ = 4096
DEPTH = 2
DEC_BATCH = 4
DEC_SEQ = 8192
PAST_LEN = 128

MIX_WIDTH = D_MODEL
HEAD_DIM = 64
N_HEADS = 8
N_KV = 2
GROUP = N_HEADS // N_KV
ATTN_WIDTH = N_HEADS * HEAD_DIM
CONV_CH = MIX_WIDTH - ATTN_WIDTH
CONV_K = 31
WINDOW = 128
BLOCK = 128
N_BUCKETS = 32
MAX_DISTANCE = 128
D_FF = 2816
FFN_CONV_K = 3
PROJ_WIDTH = ATTN_WIDTH + 2 * N_KV * HEAD_DIM + 2 * CONV_CH
EPS = 1e-6
NEG = -1e30

kernel_name = "hybrid_bidir_window_gqa_conformer_convffn"


def _rmsnorm(x, g):
    x32 = x.astype(jnp.float32)
    y = x32 * lax.rsqrt(jnp.mean(x32 * x32, axis=-1, keepdims=True) + EPS)
    return (y * g.astype(jnp.float32)).astype(x.dtype)


def _layernorm(x, g, b):
    x32 = x.astype(jnp.float32)
    mu = jnp.mean(x32, axis=-1, keepdims=True)
    xc = x32 - mu
    y = xc * lax.rsqrt(jnp.mean(xc * xc, axis=-1, keepdims=True) + EPS)
    return (y * g.astype(jnp.float32) + b.astype(jnp.float32)).astype(x.dtype)


def _dwconv(x, w, b):
    k = w.shape[0]
    pad = (k - 1) // 2
    y = lax.conv_general_dilated(
        x, w[:, None, :].astype(x.dtype), window_strides=(1,), padding=[(pad, pad)],
        dimension_numbers=("NWC", "WIO", "NWC"), feature_group_count=x.shape[-1])
    return y + b.astype(x.dtype)


def _t5_buckets(rel):
    nb = N_BUCKETS // 2
    ret = (rel > 0).astype(jnp.int32) * nb
    n = jnp.abs(rel)
    max_exact = nb // 2
    nf = jnp.maximum(n, 1).astype(jnp.float32)
    large = max_exact + (jnp.log(nf / max_exact) / math.log(MAX_DISTANCE / max_exact)
                         * (nb - max_exact)).astype(jnp.int32)
    large = jnp.minimum(large, nb - 1)
    return ret + jnp.where(n < max_exact, n, large)


def _band_bias(rel_bias):
    qi = jnp.arange(BLOCK, dtype=jnp.int32)[:, None]
    kj = jnp.arange(3 * BLOCK, dtype=jnp.int32)[None, :] - BLOCK
    rel = kj - qi
    vals = rel_bias.astype(jnp.float32)[_t5_buckets(rel)]
    vals = jnp.transpose(vals, (2, 0, 1)).reshape(N_KV, GROUP, BLOCK, 3 * BLOCK)
    band = jnp.abs(rel) <= WINDOW
    return vals, band


def _band_blocks(t, nb):
    b = t.shape[0]
    tp = jnp.pad(t, ((0, 0), (BLOCK, BLOCK), (0, 0), (0, 0))).reshape(b, nb + 2, BLOCK, N_KV, HEAD_DIM)
    return jnp.concatenate([tp[:, :-2], tp[:, 1:-1], tp[:, 2:]], axis=2)


def _window_attn(q, k, v, bias, band, sink):
    b, s, _ = q.shape
    nb = s // BLOCK
    qb = q.reshape(b, nb, BLOCK, N_KV, GROUP, HEAD_DIM)
    kb = _band_blocks(k.reshape(b, s, N_KV, HEAD_DIM), nb)
    vb = _band_blocks(v.reshape(b, s, N_KV, HEAD_DIM), nb)
    kpos = jnp.arange(nb)[:, None] * BLOCK + jnp.arange(3 * BLOCK)[None, :] - BLOCK
    kvalid = (kpos >= 0) & (kpos < s)
    mask = band[None] & kvalid[:, None, :]
    sc = jnp.einsum("bnqhgd,bnkhd->bnhgqk", qb, kb,
                    preferred_element_type=jnp.float32) * (HEAD_DIM ** -0.5)
    sc = sc + bias[None, None]
    sc = jnp.where(mask[None, :, None, None], sc, NEG)
    sk = sink.astype(jnp.float32).reshape(N_KV, GROUP, 1, 1)
    m = jnp.maximum(jnp.max(sc, axis=-1, keepdims=True), sk)
    p = jnp.exp(sc - m)
    denom = jnp.sum(p, axis=-1, keepdims=True) + jnp.exp(sk - m)
    p = (p / denom).astype(v.dtype)
    o = jnp.einsum("bnhgqk,bnkhd->bnqhgd", p, vb)
    return o.reshape(b, s, ATTN_WIDTH)


def _trunk(x, bias, band, norm_attn_g, w_in, attn_sink, conv_dw_w, conv_dw_b, conv_ln_g,
           conv_ln_b, w_out, norm_ffn_g, w_up, ffn_dw_w, ffn_dw_b, w_down, norm_final_g):
    kv_w = N_KV * HEAD_DIM
    for l in range(DEPTH):
        h = _rmsnorm(x, norm_attn_g[l])
        z = h @ w_in[l]
        q = z[..., :ATTN_WIDTH]
        k = z[..., ATTN_WIDTH:ATTN_WIDTH + kv_w]
        v = z[..., ATTN_WIDTH + kv_w:ATTN_WIDTH + 2 * kv_w]
        c0 = ATTN_WIDTH + 2 * kv_w
        ca = z[..., c0:c0 + CONV_CH]
        cb = z[..., c0 + CONV_CH:]
        a_out = _window_attn(q, k, v, bias, band, attn_sink[l])
        c = ca * jax.nn.sigmoid(cb)
        c = _dwconv(c, conv_dw_w[l], conv_dw_b[l])
        c = jax.nn.silu(_layernorm(c, conv_ln_g[l], conv_ln_b[l]))
        x = x + jnp.concatenate([a_out, c], axis=-1) @ w_out[l]
        h = _rmsnorm(x, norm_ffn_g[l])
        u = _dwconv(h @ w_up[l], ffn_dw_w[l], ffn_dw_b[l])
        x = x + (jax.nn.silu(u[..., :D_FF]) * u[..., D_FF:]) @ w_down[l]
    return _rmsnorm(x, norm_final_g)


def setup_inputs(seed: int = 0) -> dict:
    key = jax.random.key(seed)
    ks = jax.random.split(key, 20)
    f32 = jnp.float32
    nrm = lambda k, shape, scale: jax.random.normal(k, shape, f32) * scale
    return {
        "x_prompt": nrm(ks[0], (BATCH, SEQ, D_MODEL), 1.0),
        "x_sample": nrm(ks[1], (DEC_BATCH, DEC_SEQ, D_MODEL), 1.0),
        "rel_bias": nrm(ks[2], (N_BUCKETS, N_HEADS), 0.5),
        "norm_attn_g": 1.0 + nrm(ks[3], (DEPTH, D_MODEL), 0.02),
        "w_in": nrm(ks[4], (DEPTH, D_MODEL, PROJ_WIDTH), D_MODEL ** -0.5),
        "attn_sink": nrm(ks[5], (DEPTH, N_HEADS), 0.5),
        "conv_dw_w": nrm(ks[6], (DEPTH, CONV_K, CONV_CH), CONV_K ** -0.5),
        "conv_dw_b": nrm(ks[7], (DEPTH, CONV_CH), 0.02),
        "conv_ln_g": 1.0 + nrm(ks[8], (DEPTH, CONV_CH), 0.02),
        "conv_ln_b": nrm(ks[9], (DEPTH, CONV_CH), 0.02),
        "w_out": nrm(ks[10], (DEPTH, MIX_WIDTH, D_MODEL), MIX_WIDTH ** -0.5),
        "norm_ffn_g": 1.0 + nrm(ks[11], (DEPTH, D_MODEL), 0.02),
        "w_up": nrm(ks[12], (DEPTH, D_MODEL, 2 * D_FF), D_MODEL ** -0.5),
        "ffn_dw_w": nrm(ks[13], (DEPTH, FFN_CONV_K, 2 * D_FF), FFN_CONV_K ** -0.5),
        "ffn_dw_b": nrm(ks[14], (DEPTH, 2 * D_FF), 0.02),
        "w_down": nrm(ks[15], (DEPTH, D_FF, D_MODEL), D_FF ** -0.5),
        "norm_final_g": 1.0 + nrm(ks[16], (D_MODEL,), 0.02),
    }


def reference(x_prompt, x_sample, rel_bias, norm_attn_g, w_in, attn_sink, conv_dw_w, conv_dw_b,
              conv_ln_g, conv_ln_b, w_out, norm_ffn_g, w_up, ffn_dw_w, ffn_dw_b, w_down,
              norm_final_g):
    bias, band = _band_bias(rel_bias)
    y_prompt = _trunk(x_prompt, bias, band, norm_attn_g, w_in, attn_sink, conv_dw_w, conv_dw_b,
                      conv_ln_g, conv_ln_b, w_out, norm_ffn_g, w_up, ffn_dw_w, ffn_dw_b, w_down,
                      norm_final_g)
    y_sample = _trunk(x_sample, bias, band, norm_attn_g, w_in, attn_sink, conv_dw_w, conv_dw_b,
                      conv_ln_g, conv_ln_b, w_out, norm_ffn_g, w_up, ffn_dw_w, ffn_dw_b, w_down,
                      norm_final_g)
    return (y_prompt, y_sample)
```

```python
import functools
import math

import jax
import jax.numpy as jnp
from jax import lax
from jax.experimental import pallas as pl
from jax.experimental.pallas import tpu as pltpu

F32 = jnp.float32
BF16 = jnp.bfloat16

D_MODEL = 1024
HEAD_DIM = 64
N_HEADS = 8
N_KV = 2
GROUP = N_HEADS // N_KV
ATTN_WIDTH = N_HEADS * HEAD_DIM
KV_WIDTH = N_KV * HEAD_DIM
CONV_CH = D_MODEL - ATTN_WIDTH
CONV_K = 31
CONV_PAD = (CONV_K - 1) // 2
WINDOW = 128
BLOCK = 128
BAND = 3 * BLOCK
N_BUCKETS = 32
MAX_DISTANCE = 128
D_FF = 2816
FFN_CONV_K = 3
PROJ_WIDTH = ATTN_WIDTH + 2 * KV_WIDTH + 2 * CONV_CH
EPS = 1e-6
NEG = -1e30
SCALE = HEAD_DIM ** -0.5

LANES = 128
SUBLANES = 8
VMEM_LIMIT_BYTES = 56 * 1024 * 1024

TM_IN = 512
TQ = 512
TM_FF = 512
CJ = 256
NJ = D_FF // CJ
N_SLAB = CONV_CH // LANES
G_HALO = 16
U_HALO = SUBLANES
N_VARIANT = 3


def _sigmoid(x):
    return 1.0 / (1.0 + jnp.exp(-x))


def _rmsnorm(x, g):
    ms = jnp.mean(x * x, axis=-1, keepdims=True)
    return (x * lax.rsqrt(ms + EPS)) * g


def _bias_table_kernel(rb_ref, bucket_ref, valid_ref, out_ref):
    n = pl.program_id(1)
    bucket = bucket_ref[...]
    val = jnp.zeros(bucket.shape, F32)
    for b in range(N_BUCKETS):
        val = jnp.where(bucket == b, rb_ref[b, n], val)
    out_ref[...] = jnp.where(valid_ref[...] != 0, val, NEG)


def _t5_buckets(rel):
    nb = N_BUCKETS // 2
    ret = (rel > 0).astype(jnp.int32) * nb
    n = jnp.abs(rel)
    max_exact = nb // 2
    nf = jnp.maximum(n, 1).astype(F32)
    large = max_exact + (jnp.log(nf / max_exact) / math.log(MAX_DISTANCE / max_exact)
                         * (nb - max_exact)).astype(jnp.int32)
    large = jnp.minimum(large, nb - 1)
    return ret + jnp.where(n < max_exact, n, large)


def _band_tables(rel_bias):
    qi = jnp.arange(BLOCK, dtype=jnp.int32)[:, None]
    kj = jnp.arange(BAND, dtype=jnp.int32)[None, :]
    offsets = jnp.array([-BLOCK, 0, -2 * BLOCK], jnp.int32)[:, None, None]
    rel = (kj - qi)[None] + offsets
    bucket = _t5_buckets(rel)
    valid = (jnp.abs(rel) <= WINDOW).astype(jnp.int32)
    return pl.pallas_call(
        _bias_table_kernel,
        grid=(N_VARIANT, N_HEADS),
        in_specs=[
            pl.BlockSpec(memory_space=pltpu.SMEM),
            pl.BlockSpec((None, BLOCK, BAND), lambda v, n: (v, 0, 0)),
            pl.BlockSpec((None, BLOCK, BAND), lambda v, n: (v, 0, 0)),
        ],
        out_specs=pl.BlockSpec((None, None, BLOCK, BAND), lambda v, n: (v, n, 0, 0)),
        out_shape=jax.ShapeDtypeStruct((N_VARIANT, N_HEADS, BLOCK, BAND), F32),
        name="band_bias_tables",
    )(rel_bias.astype(F32), bucket, valid)


def _inproj_kernel(x_ref, g_ref, w_ref, q_ref, k_ref, v_ref, glu_ref):
    h = _rmsnorm(x_ref[...], g_ref[...]).astype(BF16)
    z = jnp.dot(h, w_ref[...], preferred_element_type=F32)
    q_ref[...] = (z[:, :ATTN_WIDTH] * SCALE).astype(BF16)
    k0 = ATTN_WIDTH
    v0 = ATTN_WIDTH + KV_WIDTH
    c0 = ATTN_WIDTH + 2 * KV_WIDTH
    kf = z[:, k0:v0]
    vf = z[:, v0:c0]
    k_ref[:, :KV_WIDTH] = kf.astype(BF16)
    k_ref[:, KV_WIDTH:] = pltpu.roll(kf, HEAD_DIM, axis=1).astype(BF16)
    v_ref[:, :KV_WIDTH] = vf.astype(BF16)
    v_ref[:, KV_WIDTH:] = pltpu.roll(vf, HEAD_DIM, axis=1).astype(BF16)
    glu = z[:, c0:c0 + CONV_CH] * _sigmoid(z[:, c0 + CONV_CH:])
    for c in range(N_SLAB):
        glu_ref[c] = glu[:, c * LANES:(c + 1) * LANES]


def _inproj(x, g, w_in_bf16):
    b, s, _ = x.shape
    tm = TM_IN
    return pl.pallas_call(
        _inproj_kernel,
        grid=(b, s // tm),
        in_specs=[
            pl.BlockSpec((None, tm, D_MODEL), lambda bi, i: (bi, i, 0)),
            pl.BlockSpec((1, D_MODEL), lambda bi, i: (0, 0)),
            pl.BlockSpec((D_MODEL, PROJ_WIDTH), lambda bi, i: (0, 0)),
        ],
        out_specs=[
            pl.BlockSpec((None, tm, ATTN_WIDTH), lambda bi, i: (bi, i, 0)),
            pl.BlockSpec((None, tm, 2 * KV_WIDTH), lambda bi, i: (bi, i, 0)),
            pl.BlockSpec((None, tm, 2 * KV_WIDTH), lambda bi, i: (bi, i, 0)),
            pl.BlockSpec((None, N_SLAB, tm, LANES), lambda bi, i: (bi, 0, i, 0)),
        ],
        out_shape=[
            jax.ShapeDtypeStruct((b, s, ATTN_WIDTH), BF16),
            jax.ShapeDtypeStruct((b, s, 2 * KV_WIDTH), BF16),
            jax.ShapeDtypeStruct((b, s, 2 * KV_WIDTH), BF16),
            jax.ShapeDtypeStruct((b, N_SLAB, s, LANES), F32),
        ],
        compiler_params=pltpu.CompilerParams(
            dimension_semantics=("arbitrary", "arbitrary"), vmem_limit_bytes=VMEM_LIMIT_BYTES),
        name="inproj",
    )(x, g, w_in_bf16)


def _mixer_kernel(sink_ref, x_ref, q_ref, k_ref, v_ref, gm_ref, gp_ref, gn_ref, bias_ref,
                  cw_ref, cb_ref, lng_ref, lnb_ref, wo_ref, out_ref,
                  gpad_ref, kz_ref, vz_ref, y_ref, conv_ref, *, seq):
    i = pl.program_id(1)
    first = i == 0
    last = i == pl.num_programs(1) - 1
    r0 = i * TQ
    kv_rows = TQ + 2 * BLOCK

    ws = pl.multiple_of(jnp.clip(r0 - BLOCK, 0, seq - kv_rows), BLOCK)
    lo = lax.broadcasted_iota(jnp.int32, (1, LANES), 1) < HEAD_DIM
    zero = jnp.zeros((), BF16)
    for src_ref, dst_ref in ((k_ref, kz_ref), (v_ref, vz_ref)):
        w = src_ref[pl.ds(ws, kv_rows), :]
        straight, swapped = w[:, :KV_WIDTH], w[:, KV_WIDTH:]
        dst_ref[0] = jnp.where(lo, straight, zero)
        dst_ref[1] = jnp.where(lo, zero, swapped)
        dst_ref[2] = jnp.where(lo, swapped, zero)
        dst_ref[3] = jnp.where(lo, zero, straight)

    def attn_block(jb, carry):
        row = pl.multiple_of(jb * BLOCK, BLOCK)
        r = r0 + row
        koff = pl.multiple_of(jnp.clip(r - BLOCK, 0, seq - BAND) - ws, BLOCK)
        var = jnp.where(r == 0, 1, jnp.where(r == seq - BLOCK, 2, 0))
        for m in range(N_HEADS // 2):
            h = m // (GROUP // 2)
            qp = q_ref[pl.ds(row, BLOCK), m * LANES:(m + 1) * LANES]
            pv = []
            ls = []
            for e in range(2):
                n = 2 * m + e
                kz = kz_ref[2 * h + e, pl.ds(koff, BAND), :]
                s = lax.dot_general(qp, kz, (((1,), (1,)), ((), ())), preferred_element_type=F32)
                s = s + bias_ref[var, n]
                sk = sink_ref[n]
                mx = jnp.maximum(jnp.max(s, axis=-1, keepdims=True), sk)
                p = jnp.exp(s - mx)
                ls.append(jnp.sum(p, axis=-1, keepdims=True) + jnp.exp(sk - mx))
                vz = vz_ref[2 * h + e, pl.ds(koff, BAND), :]
                pv.append(jnp.dot(p.astype(BF16), vz, preferred_element_type=F32))
            denom = jnp.where(lo, ls[0], ls[1])
            y_ref[pl.ds(row, BLOCK), m * LANES:(m + 1) * LANES] = ((pv[0] + pv[1]) / denom).astype(BF16)
        return carry

    lax.fori_loop(0, TQ // BLOCK, attn_block, 0)

    gpad_ref[:, :G_HALO, :] = jnp.where(first, 0.0, gp_ref[...])
    gpad_ref[:, G_HALO:G_HALO + TQ, :] = gm_ref[...]
    gpad_ref[:, G_HALO + TQ:, :] = jnp.where(last, 0.0, gn_ref[...])

    rc = 64

    def conv_chunk(ci, carry):
        row = pl.multiple_of(ci * rc, rc)
        for c in range(N_SLAB):
            lanes = slice(c * LANES, (c + 1) * LANES)
            acc = jnp.broadcast_to(cb_ref[:, lanes], (rc, LANES))
            for t in range(CONV_K):
                acc = acc + gpad_ref[c, pl.ds(row + (G_HALO - CONV_PAD) + t, rc), :] * cw_ref[t:t + 1, lanes]
            conv_ref[pl.ds(row, rc), lanes] = acc
        return carry

    lax.fori_loop(0, TQ // rc, conv_chunk, 0)

    cv = conv_ref[...]
    mu = jnp.mean(cv, axis=-1, keepdims=True)
    xc = cv - mu
    ln = (xc * lax.rsqrt(jnp.mean(xc * xc, axis=-1, keepdims=True) + EPS)) * lng_ref[...] + lnb_ref[...]
    y_ref[:, ATTN_WIDTH:] = (ln * _sigmoid(ln)).astype(BF16)

    out_ref[...] = x_ref[...] + jnp.dot(y_ref[...], wo_ref[...], preferred_element_type=F32)


def _mixer(x, q, k, v, glu, tables, sink, cw, cb, lng, lnb, wo_bf16):
    b, s, _ = x.shape
    n_tiles = s // TQ
    halo_blocks = TQ // G_HALO
    n_halo = s // G_HALO
    kernel = functools.partial(_mixer_kernel, seq=s)
    return pl.pallas_call(
        kernel,
        grid=(b, n_tiles),
        in_specs=[
            pl.BlockSpec(memory_space=pltpu.SMEM),
            pl.BlockSpec((None, TQ, D_MODEL), lambda bi, i: (bi, i, 0)),
            pl.BlockSpec((None, TQ, ATTN_WIDTH), lambda bi, i: (bi, i, 0)),
            pl.BlockSpec((None, s, 2 * KV_WIDTH), lambda bi, i: (bi, 0, 0)),
            pl.BlockSpec((None, s, 2 * KV_WIDTH), lambda bi, i: (bi, 0, 0)),
            pl.BlockSpec((None, N_SLAB, TQ, LANES), lambda bi, i: (bi, 0, i, 0)),
            pl.BlockSpec((None, N_SLAB, G_HALO, LANES),
                         lambda bi, i: (bi, 0, jnp.maximum(i * halo_blocks - 1, 0), 0)),
            pl.BlockSpec((None, N_SLAB, G_HALO, LANES),
                         lambda bi, i: (bi, 0, jnp.minimum((i + 1) * halo_blocks, n_halo - 1), 0)),
            pl.BlockSpec((N_VARIANT, N_HEADS, BLOCK, BAND), lambda bi, i: (0, 0, 0, 0)),
            pl.BlockSpec((CONV_K, CONV_CH), lambda bi, i: (0, 0)),
            pl.BlockSpec((1, CONV_CH), lambda bi, i: (0, 0)),
            pl.BlockSpec((1, CONV_CH), lambda bi, i: (0, 0)),
            pl.BlockSpec((1, CONV_CH), lambda bi, i: (0, 0)),
            pl.BlockSpec((D_MODEL, D_MODEL), lambda bi, i: (0, 0)),
        ],
        out_specs=pl.BlockSpec((None, TQ, D_MODEL), lambda bi, i: (bi, i, 0)),
        out_shape=jax.ShapeDtypeStruct((b, s, D_MODEL), F32),
        scratch_shapes=[
            pltpu.VMEM((N_SLAB, TQ + 2 * G_HALO, LANES), F32),
            pltpu.VMEM((2 * N_KV, TQ + 2 * BLOCK, LANES), BF16),
            pltpu.VMEM((2 * N_KV, TQ + 2 * BLOCK, LANES), BF16),
            pltpu.VMEM((TQ, D_MODEL), BF16),
            pltpu.VMEM((TQ, CONV_CH), F32),
        ],
        compiler_params=pltpu.CompilerParams(
            dimension_semantics=("arbitrary", "arbitrary"), vmem_limit_bytes=VMEM_LIMIT_BYTES),
        name="mixer",
    )(sink, x, q, k, v, glu, glu, glu, tables, cw, cb, lng, lnb, wo_bf16)


def _ffn_kernel(x_ref, xp_ref, xn_ref, g_ref, wup_ref, dw_ref, db_ref, wdn_ref, gf_ref, out_ref,
                hb_ref, ug_ref, uv_ref, acc_ref, *, final_norm):
    i = pl.program_id(1)
    first = i == 0
    last = i == pl.num_programs(1) - 1
    tm = TM_FF
    g = g_ref[...]
    hb_ref[:U_HALO, :] = jnp.where(first, 0.0, _rmsnorm(xp_ref[...], g)).astype(BF16)
    hb_ref[U_HALO:U_HALO + tm, :] = _rmsnorm(x_ref[...], g).astype(BF16)
    hb_ref[U_HALO + tm:, :] = jnp.where(last, 0.0, _rmsnorm(xn_ref[...], g)).astype(BF16)

    def conv3(u_ref, c, w, bias):
        lanes = slice(c * LANES, (c + 1) * LANES)
        out = bias[:, lanes] + u_ref[c, U_HALO - 1:U_HALO - 1 + tm, :] * w[0:1, lanes]
        out = out + u_ref[c, U_HALO:U_HALO + tm, :] * w[1:2, lanes]
        return out + u_ref[c, U_HALO + 1:U_HALO + 1 + tm, :] * w[2:3, lanes]

    def chunk(j, carry):
        hb = hb_ref[...]
        ug = jnp.dot(hb, wup_ref[0, j], preferred_element_type=F32)
        uv = jnp.dot(hb, wup_ref[1, j], preferred_element_type=F32)
        for c in range(CJ // LANES):
            ug_ref[c] = ug[:, c * LANES:(c + 1) * LANES]
            uv_ref[c] = uv[:, c * LANES:(c + 1) * LANES]
        acts = []
        for c in range(CJ // LANES):
            gate = conv3(ug_ref, c, dw_ref[0, j], db_ref[0, j])
            val = conv3(uv_ref, c, dw_ref[1, j], db_ref[1, j])
            acts.append(((gate * _sigmoid(gate)) * val).astype(BF16))
        act = jnp.concatenate(acts, axis=1)
        part = jnp.dot(act, wdn_ref[j], preferred_element_type=F32)

        @pl.when(j == 0)
        def _():
            acc_ref[...] = part

        @pl.when(j > 0)
        def _():
            acc_ref[...] += part

        return carry

    lax.fori_loop(0, NJ, chunk, 0)
    y = x_ref[...] + acc_ref[...]
    if final_norm:
        y = _rmsnorm(y, gf_ref[...])
    out_ref[...] = y


def _ffn(x, g, wup, dw, db, wdn, gf, final_norm):
    b, s, _ = x.shape
    tm = TM_FF
    halo_blocks = tm // U_HALO
    n_halo = s // U_HALO
    kernel = functools.partial(_ffn_kernel, final_norm=final_norm)
    return pl.pallas_call(
        kernel,
        grid=(b, s // tm),
        in_specs=[
            pl.BlockSpec((None, tm, D_MODEL), lambda bi, i: (bi, i, 0)),
            pl.BlockSpec((None, U_HALO, D_MODEL), lambda bi, i: (bi, jnp.maximum(i * halo_blocks - 1, 0), 0)),
            pl.BlockSpec((None, U_HALO, D_MODEL),
                         lambda bi, i: (bi, jnp.minimum((i + 1) * halo_blocks, n_halo - 1), 0)),
            pl.BlockSpec((1, D_MODEL), lambda bi, i: (0, 0)),
            pl.BlockSpec((2, NJ, D_MODEL, CJ), lambda bi, i: (0, 0, 0, 0)),
            pl.BlockSpec((2, NJ, FFN_CONV_K, CJ), lambda bi, i: (0, 0, 0, 0)),
            pl.BlockSpec((2, NJ, 1, CJ), lambda bi, i: (0, 0, 0, 0)),
            pl.BlockSpec((NJ, CJ, D_MODEL), lambda bi, i: (0, 0, 0)),
            pl.BlockSpec((1, D_MODEL), lambda bi, i: (0, 0)),
        ],
        out_specs=pl.BlockSpec((None, tm, D_MODEL), lambda bi, i: (bi, i, 0)),
        out_shape=jax.ShapeDtypeStruct((b, s, D_MODEL), F32),
        scratch_shapes=[
            pltpu.VMEM((tm + 2 * U_HALO, D_MODEL), BF16),
            pltpu.VMEM((CJ // LANES, tm + 2 * U_HALO, LANES), F32),
            pltpu.VMEM((CJ // LANES, tm + 2 * U_HALO, LANES), F32),
            pltpu.VMEM((tm, D_MODEL), F32),
        ],
        compiler_params=pltpu.CompilerParams(
            dimension_semantics=("arbitrary", "arbitrary"), vmem_limit_bytes=VMEM_LIMIT_BYTES),
        name="ffn",
    )(x, x, x, g, wup, dw, db, wdn, gf)


def _split_ff(a):
    lead = a.shape[:-1]
    a = a.reshape(lead + (2, NJ, CJ))
    return jnp.moveaxis(a, (-3, -2), (0, 1))


def _prepare_layer(l, norm_attn_g, w_in, attn_sink, conv_dw_w, conv_dw_b, conv_ln_g, conv_ln_b, w_out,
                   norm_ffn_g, w_up, ffn_dw_w, ffn_dw_b, w_down):
    return dict(
        g_attn=norm_attn_g[l].astype(F32)[None, :],
        w_in=w_in[l].astype(BF16),
        sink=attn_sink[l].astype(F32),
        cw=conv_dw_w[l].astype(F32),
        cb=conv_dw_b[l].astype(F32)[None, :],
        lng=conv_ln_g[l].astype(F32)[None, :],
        lnb=conv_ln_b[l].astype(F32)[None, :],
        w_out=w_out[l].astype(BF16),
        g_ffn=norm_ffn_g[l].astype(F32)[None, :],
        wup=_split_ff(w_up[l].astype(BF16)),
        dw=_split_ff(ffn_dw_w[l].astype(F32)),
        db=_split_ff(ffn_dw_b[l].astype(F32)[None, :]),
        wdn=w_down[l].astype(BF16).reshape(NJ, CJ, D_MODEL),
    )


def _trunk(x, tables, layers, gf):
    depth = len(layers)
    for l, p in enumerate(layers):
        q, k, v, glu = _inproj(x, p["g_attn"], p["w_in"])
        x = _mixer(x, q, k, v, glu, tables, p["sink"], p["cw"], p["cb"], p["lng"], p["lnb"], p["w_out"])
        x = _ffn(x, p["g_ffn"], p["wup"], p["dw"], p["db"], p["wdn"], gf, final_norm=(l == depth - 1))
    return x


def kernel(x_prompt, x_sample, rel_bias, norm_attn_g, w_in, attn_sink, conv_dw_w, conv_dw_b, conv_ln_g,
           conv_ln_b, w_out, norm_ffn_g, w_up, ffn_dw_w, ffn_dw_b, w_down, norm_final_g):
    depth = w_in.shape[0]
    tables = _band_tables(rel_bias)
    layers = [
        _prepare_layer(l, norm_attn_g, w_in, attn_sink, conv_dw_w, conv_dw_b, conv_ln_g, conv_ln_b, w_out,
                       norm_ffn_g, w_up, ffn_dw_w, ffn_dw_b, w_down)
        for l in range(depth)
    ]
    gf = norm_final_g.astype(F32)[None, :]
    return (_trunk(x_prompt, tables, layers, gf), _trunk(x_sample, tables, layers, gf))
```

```python
import functools
import math

import jax
import jax.numpy as jnp
from jax import lax
from jax.experimental import pallas as pl
from jax.experimental.pallas import tpu as pltpu

F32 = jnp.float32
BF16 = jnp.bfloat16

D_MODEL = 1024
HEAD_DIM = 64
N_HEADS = 8
N_KV = 2
GROUP = N_HEADS // N_KV
ATTN_WIDTH = N_HEADS * HEAD_DIM
KV_WIDTH = N_KV * HEAD_DIM
CONV_CH = D_MODEL - ATTN_WIDTH
CONV_K = 31
CONV_PAD = (CONV_K - 1) // 2
WINDOW = 128
BLOCK = 128
BAND = 3 * BLOCK
N_BUCKETS = 32
MAX_DISTANCE = 128
D_FF = 2816
FFN_CONV_K = 3
PROJ_WIDTH = ATTN_WIDTH + 2 * KV_WIDTH + 2 * CONV_CH
EPS = 1e-6
NEG = -1e30
SCALE = HEAD_DIM ** -0.5

LANES = 128
SUBLANES = 8
VMEM_LIMIT_BYTES = 56 * 1024 * 1024

TM_IN = 512
TQ = 512
TM_FF = 512
CJ = 256
NJ = D_FF // CJ
N_SLAB = CONV_CH // LANES
G_HALO = 16
U_HALO = SUBLANES
N_VARIANT = 3


def _sigmoid(x):
    return 1.0 / (1.0 + jnp.exp(-x))


def _rmsnorm(x, g):
    ms = jnp.mean(x * x, axis=-1, keepdims=True)
    return (x * lax.rsqrt(ms + EPS)) * g


def _bias_table_kernel(rb_ref, bucket_ref, valid_ref, out_ref):
    n = pl.program_id(1)
    bucket = bucket_ref[...]
    val = jnp.zeros(bucket.shape, F32)
    for b in range(N_BUCKETS):
        val = jnp.where(bucket == b, rb_ref[b, n], val)
    out_ref[...] = jnp.where(valid_ref[...] != 0, val, NEG)


def _t5_buckets(rel):
    nb = N_BUCKETS // 2
    ret = (rel > 0).astype(jnp.int32) * nb
    n = jnp.abs(rel)
    max_exact = nb // 2
    nf = jnp.maximum(n, 1).astype(F32)
    large = max_exact + (jnp.log(nf / max_exact) / math.log(MAX_DISTANCE / max_exact)
                         * (nb - max_exact)).astype(jnp.int32)
    large = jnp.minimum(large, nb - 1)
    return ret + jnp.where(n < max_exact, n, large)


def _band_tables(rel_bias):
    qi = jnp.arange(BLOCK, dtype=jnp.int32)[:, None]
    kj = jnp.arange(BAND, dtype=jnp.int32)[None, :]
    offsets = jnp.array([-BLOCK, 0, -2 * BLOCK], jnp.int32)[:, None, None]
    rel = (kj - qi)[None] + offsets
    bucket = _t5_buckets(rel)
    valid = (jnp.abs(rel) <= WINDOW).astype(jnp.int32)
    return pl.pallas_call(
        _bias_table_kernel,
        grid=(N_VARIANT, N_HEADS),
        in_specs=[
            pl.BlockSpec(memory_space=pltpu.SMEM),
            pl.BlockSpec((None, BLOCK, BAND), lambda v, n: (v, 0, 0)),
            pl.BlockSpec((None, BLOCK, BAND), lambda v, n: (v, 0, 0)),
        ],
        out_specs=pl.BlockSpec((None, None, BLOCK, BAND), lambda v, n: (v, n, 0, 0)),
        out_shape=jax.ShapeDtypeStruct((N_VARIANT, N_HEADS, BLOCK, BAND), F32),
        name="band_bias_tables",
    )(rel_bias.astype(F32), bucket, valid)


def _inproj_kernel(x_ref, g_ref, w_ref, q_ref, k_ref, v_ref, glu_ref):
    h = _rmsnorm(x_ref[...], g_ref[...]).astype(BF16)
    z = jnp.dot(h, w_ref[...], preferred_element_type=F32)
    q_ref[...] = (z[:, :ATTN_WIDTH] * SCALE).astype(BF16)
    k0 = ATTN_WIDTH
    v0 = ATTN_WIDTH + KV_WIDTH
    c0 = ATTN_WIDTH + 2 * KV_WIDTH
    kf = z[:, k0:v0]
    vf = z[:, v0:c0]
    k_ref[:, :KV_WIDTH] = kf.astype(BF16)
    k_ref[:, KV_WIDTH:] = pltpu.roll(kf, HEAD_DIM, axis=1).astype(BF16)
    v_ref[:, :KV_WIDTH] = vf.astype(BF16)
    v_ref[:, KV_WIDTH:] = pltpu.roll(vf, HEAD_DIM, axis=1).astype(BF16)
    glu = z[:, c0:c0 + CONV_CH] * _sigmoid(z[:, c0 + CONV_CH:])
    for c in range(N_SLAB):
        glu_ref[c] = glu[:, c * LANES:(c + 1) * LANES]


def _inproj(x, g, w_in_bf16):
    b, s, _ = x.shape
    tm = TM_IN
    return pl.pallas_call(
        _inproj_kernel,
        grid=(b, s // tm),
        in_specs=[
            pl.BlockSpec((None, tm, D_MODEL), lambda bi, i: (bi, i, 0)),
            pl.BlockSpec((1, D_MODEL), lambda bi, i: (0, 0)),
            pl.BlockSpec((D_MODEL, PROJ_WIDTH), lambda bi, i: (0, 0)),
        ],
        out_specs=[
            pl.BlockSpec((None, tm, ATTN_WIDTH), lambda bi, i: (bi, i, 0)),
            pl.BlockSpec((None, tm, 2 * KV_WIDTH), lambda bi, i: (bi, i, 0)),
            pl.BlockSpec((None, tm, 2 * KV_WIDTH), lambda bi, i: (bi, i, 0)),
            pl.BlockSpec((None, N_SLAB, tm, LANES), lambda bi, i: (bi, 0, i, 0)),
        ],
        out_shape=[
            jax.ShapeDtypeStruct((b, s, ATTN_WIDTH), BF16),
            jax.ShapeDtypeStruct((b, s, 2 * KV_WIDTH), BF16),
            jax.ShapeDtypeStruct((b, s, 2 * KV_WIDTH), BF16),
            jax.ShapeDtypeStruct((b, N_SLAB, s, LANES), F32),
        ],
        compiler_params=pltpu.CompilerParams(
            dimension_semantics=("arbitrary", "arbitrary"), vmem_limit_bytes=VMEM_LIMIT_BYTES),
        name="inproj",
    )(x, g, w_in_bf16)


def _mixer_kernel(sink_ref, x_ref, q_ref, k_ref, v_ref, gm_ref, gp_ref, gn_ref, bias_ref,
                  cw_ref, cb_ref, lng_ref, lnb_ref, wo_ref, out_ref,
                  gpad_ref, kz_ref, vz_ref, y_ref, conv_ref, *, seq):
    i = pl.program_id(1)
    first = i == 0
    last = i == pl.num_programs(1) - 1
    r0 = i * TQ
    kv_rows = TQ + 2 * BLOCK

    ws = pl.multiple_of(jnp.clip(r0 - BLOCK, 0, seq - kv_rows), BLOCK)
    lo = lax.broadcasted_iota(jnp.int32, (1, LANES), 1) < HEAD_DIM
    zero = jnp.zeros((), BF16)
    for src_ref, dst_ref in ((k_ref, kz_ref), (v_ref, vz_ref)):
        w = src_ref[pl.ds(ws, kv_rows), :]
        straight, swapped = w[:, :KV_WIDTH], w[:, KV_WIDTH:]
        dst_ref[0] = jnp.where(lo, straight, zero)
        dst_ref[1] = jnp.where(lo, zero, swapped)
        dst_ref[2] = jnp.where(lo, swapped, zero)
        dst_ref[3] = jnp.where(lo, zero, straight)

    def attn_block(jb, carry):
        row = pl.multiple_of(jb * BLOCK, BLOCK)
        r = r0 + row
        koff = pl.multiple_of(jnp.clip(r - BLOCK, 0, seq - BAND) - ws, BLOCK)
        var = jnp.where(r == 0, 1, jnp.where(r == seq - BLOCK, 2, 0))
        for m in range(N_HEADS // 2):
            h = m // (GROUP // 2)
            qp = q_ref[pl.ds(row, BLOCK), m * LANES:(m + 1) * LANES]
            pv = []
            ls = []
            for e in range(2):
                n = 2 * m + e
                kz = kz_ref[2 * h + e, pl.ds(koff, BAND), :]
                s = lax.dot_general(qp, kz, (((1,), (1,)), ((), ())), preferred_element_type=F32)
                s = s + bias_ref[var, n]
                sk = sink_ref[n]
                mx = jnp.maximum(jnp.max(s, axis=-1, keepdims=True), sk)
                p = jnp.exp(s - mx)
                ls.append(jnp.sum(p, axis=-1, keepdims=True) + jnp.exp(sk - mx))
                vz = vz_ref[2 * h + e, pl.ds(koff, BAND), :]
                pv.append(jnp.dot(p.astype(BF16), vz, preferred_element_type=F32))
            denom = jnp.where(lo, ls[0], ls[1])
            y_ref[pl.ds(row, BLOCK), m * LANES:(m + 1) * LANES] = ((pv[0] + pv[1]) / denom).astype(BF16)
        return carry

    lax.fori_loop(0, TQ // BLOCK, attn_block, 0)

    gpad_ref[:, :G_HALO, :] = jnp.where(first, 0.0, gp_ref[...])
    gpad_ref[:, G_HALO:G_HALO + TQ, :] = gm_ref[...]
    gpad_ref[:, G_HALO + TQ:, :] = jnp.where(last, 0.0, gn_ref[...])

    rc = 64

    def conv_chunk(ci, carry):
        row = pl.multiple_of(ci * rc, rc)
        for c in range(N_SLAB):
            lanes = slice(c * LANES, (c + 1) * LANES)
            acc = jnp.broadcast_to(cb_ref[:, lanes], (rc, LANES))
            for t in range(CONV_K):
                acc = acc + gpad_ref[c, pl.ds(row + (G_HALO - CONV_PAD) + t, rc), :] * cw_ref[t:t + 1, lanes]
            conv_ref[pl.ds(row, rc), lanes] = acc
        return carry

    lax.fori_loop(0, TQ // rc, conv_chunk, 0)

    cv = conv_ref[...]
    mu = jnp.mean(cv, axis=-1, keepdims=True)
    xc = cv - mu
    ln = (xc * lax.rsqrt(jnp.mean(xc * xc, axis=-1, keepdims=True) + EPS)) * lng_ref[...] + lnb_ref[...]
    y_ref[:, ATTN_WIDTH:] = (ln * _sigmoid(ln)).astype(BF16)

    out_ref[...] = x_ref[...] + jnp.dot(y_ref[...], wo_ref[...], preferred_element_type=F32)


def _mixer(x, q, k, v, glu, tables, sink, cw, cb, lng, lnb, wo_bf16):
    b, s, _ = x.shape
    n_tiles = s // TQ
    halo_blocks = TQ // G_HALO
    n_halo = s // G_HALO
    kernel = functools.partial(_mixer_kernel, seq=s)
    return pl.pallas_call(
        kernel,
        grid=(b, n_tiles),
        in_specs=[
            pl.BlockSpec(memory_space=pltpu.SMEM),
            pl.BlockSpec((None, TQ, D_MODEL), lambda bi, i: (bi, i, 0)),
            pl.BlockSpec((None, TQ, ATTN_WIDTH), lambda bi, i: (bi, i, 0)),
            pl.BlockSpec((None, s, 2 * KV_WIDTH), lambda bi, i: (bi, 0, 0)),
            pl.BlockSpec((None, s, 2 * KV_WIDTH), lambda bi, i: (bi, 0, 0)),
            pl.BlockSpec((None, N_SLAB, TQ, LANES), lambda bi, i: (bi, 0, i, 0)),
            pl.BlockSpec((None, N_SLAB, G_HALO, LANES),
                         lambda bi, i: (bi, 0, jnp.maximum(i * halo_blocks - 1, 0), 0)),
            pl.BlockSpec((None, N_SLAB, G_HALO, LANES),
                         lambda bi, i: (bi, 0, jnp.minimum((i + 1) * halo_blocks, n_halo - 1), 0)),
            pl.BlockSpec((N_VARIANT, N_HEADS, BLOCK, BAND), lambda bi, i: (0, 0, 0, 0)),
            pl.BlockSpec((CONV_K, CONV_CH), lambda bi, i: (0, 0)),
            pl.BlockSpec((1, CONV_CH), lambda bi, i: (0, 0)),
            pl.BlockSpec((1, CONV_CH), lambda bi, i: (0, 0)),
            pl.BlockSpec((1, CONV_CH), lambda bi, i: (0, 0)),
            pl.BlockSpec((D_MODEL, D_MODEL), lambda bi, i: (0, 0)),
        ],
        out_specs=pl.BlockSpec((None, TQ, D_MODEL), lambda bi, i: (bi, i, 0)),
        out_shape=jax.ShapeDtypeStruct((b, s, D_MODEL), F32),
        scratch_shapes=[
            pltpu.VMEM((N_SLAB, TQ + 2 * G_HALO, LANES), F32),
            pltpu.VMEM((2 * N_KV, TQ + 2 * BLOCK, LANES), BF16),
            pltpu.VMEM((2 * N_KV, TQ + 2 * BLOCK, LANES), BF16),
            pltpu.VMEM((TQ, D_MODEL), BF16),
            pltpu.VMEM((TQ, CONV_CH), F32),
        ],
        compiler_params=pltpu.CompilerParams(
            dimension_semantics=("arbitrary", "arbitrary"), vmem_limit_bytes=VMEM_LIMIT_BYTES),
        name="mixer",
    )(sink, x, q, k, v, glu, glu, glu, tables, cw, cb, lng, lnb, wo_bf16)


def _ffn_kernel(x_ref, xp_ref, xn_ref, g_ref, wup_ref, dw_ref, db_ref, wdn_ref, gf_ref, out_ref,
                hb_ref, uga_ref, uva_ref, ugb_ref, uvb_ref, *, final_norm):
    i = pl.program_id(1)
    first = i == 0
    last = i == pl.num_programs(1) - 1
    tm = TM_FF
    g = g_ref[...]
    hb_ref[:U_HALO, :] = jnp.where(first, 0.0, _rmsnorm(xp_ref[...], g)).astype(BF16)
    hb_ref[U_HALO:U_HALO + tm, :] = _rmsnorm(x_ref[...], g).astype(BF16)
    hb_ref[U_HALO + tm:, :] = jnp.where(last, 0.0, _rmsnorm(xn_ref[...], g)).astype(BF16)
    out_ref[...] = x_ref[...]

    def up_proj(j, ug_ref, uv_ref):
        hb = hb_ref[...]
        ug = jnp.dot(hb, wup_ref[0, j], preferred_element_type=F32)
        uv = jnp.dot(hb, wup_ref[1, j], preferred_element_type=F32)
        for c in range(CJ // LANES):
            ug_ref[c] = ug[:, c * LANES:(c + 1) * LANES]
            uv_ref[c] = uv[:, c * LANES:(c + 1) * LANES]

    def conv3(u_ref, c, w, bias):
        lanes = slice(c * LANES, (c + 1) * LANES)
        out = bias[:, lanes] + u_ref[c, U_HALO - 1:U_HALO - 1 + tm, :] * w[0:1, lanes]
        out = out + u_ref[c, U_HALO:U_HALO + tm, :] * w[1:2, lanes]
        return out + u_ref[c, U_HALO + 1:U_HALO + 1 + tm, :] * w[2:3, lanes]

    def down_proj(j, ug_ref, uv_ref):
        acts = []
        for c in range(CJ // LANES):
            gate = conv3(ug_ref, c, dw_ref[0, j], db_ref[0, j])
            val = conv3(uv_ref, c, dw_ref[1, j], db_ref[1, j])
            acts.append(((gate * _sigmoid(gate)) * val).astype(BF16))
        act = jnp.concatenate(acts, axis=1)
        out_ref[...] += jnp.dot(act, wdn_ref[j], preferred_element_type=F32)

    up_proj(0, uga_ref, uva_ref)

    def chunk_pair(p, carry):
        j = 2 * p
        up_proj(j + 1, ugb_ref, uvb_ref)
        down_proj(j, uga_ref, uva_ref)
        up_proj(j + 2, uga_ref, uva_ref)
        down_proj(j + 1, ugb_ref, uvb_ref)
        return carry

    lax.fori_loop(0, (NJ - 1) // 2, chunk_pair, 0)
    down_proj(NJ - 1, uga_ref, uva_ref)
    if final_norm:
        out_ref[...] = _rmsnorm(out_ref[...], gf_ref[...])


def _resident(shape):
    return pl.BlockSpec(shape, lambda bi, i: (0,) * len(shape), pipeline_mode=pl.Buffered(1))


def _ffn(x, g, wup, dw, db, wdn, gf, final_norm):
    b, s, _ = x.shape
    tm = TM_FF
    halo_blocks = tm // U_HALO
    n_halo = s // U_HALO
    kernel = functools.partial(_ffn_kernel, final_norm=final_norm)
    return pl.pallas_call(
        kernel,
        grid=(b, s // tm),
        in_specs=[
            pl.BlockSpec((None, tm, D_MODEL), lambda bi, i: (bi, i, 0)),
            pl.BlockSpec((None, U_HALO, D_MODEL), lambda bi, i: (bi, jnp.maximum(i * halo_blocks - 1, 0), 0)),
            pl.BlockSpec((None, U_HALO, D_MODEL),
                         lambda bi, i: (bi, jnp.minimum((i + 1) * halo_blocks, n_halo - 1), 0)),
            _resident((1, D_MODEL)),
            _resident((2, NJ, D_MODEL, CJ)),
            _resident((2, NJ, FFN_CONV_K, CJ)),
            _resident((2, NJ, 1, CJ)),
            _resident((NJ, CJ, D_MODEL)),
            _resident((1, D_MODEL)),
        ],
        out_specs=pl.BlockSpec((None, tm, D_MODEL), lambda bi, i: (bi, i, 0)),
        out_shape=jax.ShapeDtypeStruct((b, s, D_MODEL), F32),
        scratch_shapes=[pltpu.VMEM((tm + 2 * U_HALO, D_MODEL), BF16)]
        + [pltpu.VMEM((CJ // LANES, tm + 2 * U_HALO, LANES), F32)] * 4,
        compiler_params=pltpu.CompilerParams(
            dimension_semantics=("arbitrary", "arbitrary"), vmem_limit_bytes=VMEM_LIMIT_BYTES),
        name="ffn",
    )(x, x, x, g, wup, dw, db, wdn, gf)


def _split_ff(a):
    lead = a.shape[:-1]
    a = a.reshape(lead + (2, NJ, CJ))
    return jnp.moveaxis(a, (-3, -2), (0, 1))


def _prepare_layer(l, norm_attn_g, w_in, attn_sink, conv_dw_w, conv_dw_b, conv_ln_g, conv_ln_b, w_out,
                   norm_ffn_g, w_up, ffn_dw_w, ffn_dw_b, w_down):
    return dict(
        g_attn=norm_attn_g[l].astype(F32)[None, :],
        w_in=w_in[l].astype(BF16),
        sink=attn_sink[l].astype(F32),
        cw=conv_dw_w[l].astype(F32),
        cb=conv_dw_b[l].astype(F32)[None, :],
        lng=conv_ln_g[l].astype(F32)[None, :],
        lnb=conv_ln_b[l].astype(F32)[None, :],
        w_out=w_out[l].astype(BF16),
        g_ffn=norm_ffn_g[l].astype(F32)[None, :],
        wup=_split_ff(w_up[l].astype(BF16)),
        dw=_split_ff(ffn_dw_w[l].astype(F32)),
        db=_split_ff(ffn_dw_b[l].astype(F32)[None, :]),
        wdn=w_down[l].astype(BF16).reshape(NJ, CJ, D_MODEL),
    )


def _trunk(x, tables, layers, gf):
    depth = len(layers)
    for l, p in enumerate(layers):
        q, k, v, glu = _inproj(x, p["g_attn"], p["w_in"])
        x = _mixer(x, q, k, v, glu, tables, p["sink"], p["cw"], p["cb"], p["lng"], p["lnb"], p["w_out"])
        x = _ffn(x, p["g_ffn"], p["wup"], p["dw"], p["db"], p["wdn"], gf, final_norm=(l == depth - 1))
    return x


def kernel(x_prompt, x_sample, rel_bias, norm_attn_g, w_in, attn_sink, conv_dw_w, conv_dw_b, conv_ln_g,
           conv_ln_b, w_out, norm_ffn_g, w_up, ffn_dw_w, ffn_dw_b, w_down, norm_final_g):
    depth = w_in.shape[0]
    tables = _band_tables(rel_bias)
    layers = [
        _prepare_layer(l, norm_attn_g, w_in, attn_sink, conv_dw_w, conv_dw_b, conv_ln_g, conv_ln_b, w_out,
                       norm_ffn_g, w_up, ffn_dw_w, ffn_dw_b, w_down)
        for l in range(depth)
    ]
    gf = norm_final_g.astype(F32)[None, :]
    return (_trunk(x_prompt, tables, layers, gf), _trunk(x_sample, tables, layers, gf))
```

```python
import functools
import math

import jax
import jax.numpy as jnp
from jax import lax
from jax.experimental import pallas as pl
from jax.experimental.pallas import tpu as pltpu

F32 = jnp.float32
BF16 = jnp.bfloat16

D_MODEL = 1024
HEAD_DIM = 64
N_HEADS = 8
N_KV = 2
GROUP = N_HEADS // N_KV
ATTN_WIDTH = N_HEADS * HEAD_DIM
KV_WIDTH = N_KV * HEAD_DIM
CONV_CH = D_MODEL - ATTN_WIDTH
CONV_K = 31
CONV_PAD = (CONV_K - 1) // 2
WINDOW = 128
BLOCK = 128
BAND = 3 * BLOCK
N_BUCKETS = 32
MAX_DISTANCE = 128
D_FF = 2816
FFN_CONV_K = 3
QKV_WIDTH = ATTN_WIDTH + 2 * KV_WIDTH
EPS = 1e-6
NEG = -1e30
SCALE = HEAD_DIM ** -0.5

LANES = 128
SUBLANES = 8
VMEM_LIMIT_BYTES = 56 * 1024 * 1024

TM_IN = 512
TQ = 512
TM_FF = 1024
CJ = 256
NJ = D_FF // CJ
N_SLAB = CONV_CH // LANES
SB = 128
CONV_ROWS = 64
G_HALO = 16
U_HALO = SUBLANES
N_VARIANT = 3


def _sigmoid(x):
    return 1.0 / (1.0 + jnp.exp(-x))


def _rmsnorm(x, g):
    ms = jnp.mean(x * x, axis=-1, keepdims=True)
    return (x * lax.rsqrt(ms + EPS)) * g


def _resident(shape):
    return pl.BlockSpec(shape, lambda bi, i: (0,) * len(shape), pipeline_mode=pl.Buffered(1))


def _halo_specs(tm, halo, seq):
    per_tile = tm // halo
    n_halo = seq // halo
    return [
        pl.BlockSpec((None, tm, D_MODEL), lambda bi, i: (bi, i, 0)),
        pl.BlockSpec((None, halo, D_MODEL), lambda bi, i: (bi, jnp.maximum(i * per_tile - 1, 0), 0)),
        pl.BlockSpec((None, halo, D_MODEL), lambda bi, i: (bi, jnp.minimum((i + 1) * per_tile, n_halo - 1), 0)),
    ]


def _bias_table_kernel(rb_ref, bucket_ref, valid_ref, out_ref):
    n = pl.program_id(1)
    bucket = bucket_ref[...]
    val = jnp.zeros(bucket.shape, F32)
    for b in range(N_BUCKETS):
        val = jnp.where(bucket == b, rb_ref[b, n], val)
    out_ref[...] = jnp.where(valid_ref[...] != 0, val, NEG)


def _t5_buckets(rel):
    nb = N_BUCKETS // 2
    ret = (rel > 0).astype(jnp.int32) * nb
    n = jnp.abs(rel)
    max_exact = nb // 2
    nf = jnp.maximum(n, 1).astype(F32)
    large = max_exact + (jnp.log(nf / max_exact) / math.log(MAX_DISTANCE / max_exact)
                         * (nb - max_exact)).astype(jnp.int32)
    large = jnp.minimum(large, nb - 1)
    return ret + jnp.where(n < max_exact, n, large)


def _band_tables(rel_bias):
    qi = jnp.arange(BLOCK, dtype=jnp.int32)[:, None]
    kj = jnp.arange(BAND, dtype=jnp.int32)[None, :]
    offsets = jnp.array([-BLOCK, 0, -2 * BLOCK], jnp.int32)[:, None, None]
    rel = (kj - qi)[None] + offsets
    bucket = _t5_buckets(rel)
    valid = (jnp.abs(rel) <= WINDOW).astype(jnp.int32)
    return pl.pallas_call(
        _bias_table_kernel,
        grid=(N_VARIANT, N_HEADS),
        in_specs=[
            pl.BlockSpec(memory_space=pltpu.SMEM),
            pl.BlockSpec((None, BLOCK, BAND), lambda v, n: (v, 0, 0)),
            pl.BlockSpec((None, BLOCK, BAND), lambda v, n: (v, 0, 0)),
        ],
        out_specs=pl.BlockSpec((None, None, BLOCK, BAND), lambda v, n: (v, n, 0, 0)),
        out_shape=jax.ShapeDtypeStruct((N_VARIANT, N_HEADS, BLOCK, BAND), F32),
        name="band_bias_tables",
    )(rel_bias.astype(F32), bucket, valid)


def _inproj_kernel(x_ref, xp_ref, xn_ref, g_ref, wqkv_ref, wc_ref, cw_ref, cb_ref, lng_ref, lnb_ref,
                   q_ref, k_ref, v_ref, c_ref, hb_ref, gl_ref, cv_ref):
    i = pl.program_id(1)
    first = i == 0
    last = i == pl.num_programs(1) - 1
    tm = TM_IN
    g = g_ref[...]

    def norm_rows(lo, hi):
        if lo < G_HALO:
            hb_ref[lo:G_HALO, :] = jnp.where(first, 0.0, _rmsnorm(xp_ref[lo:, :], g)).astype(BF16)
        m_lo, m_hi = max(lo, G_HALO), min(hi, G_HALO + tm)
        hb_ref[m_lo:m_hi, :] = _rmsnorm(x_ref[m_lo - G_HALO:m_hi - G_HALO, :], g).astype(BF16)
        if hi > G_HALO + tm:
            hb_ref[G_HALO + tm:hi, :] = jnp.where(last, 0.0, _rmsnorm(xn_ref[:hi - G_HALO - tm, :], g)).astype(BF16)

    def glu_rows(lo, hi):
        zc = jnp.dot(hb_ref[lo:hi, :], wc_ref[...], preferred_element_type=F32)
        glu = zc[:, :CONV_CH] * _sigmoid(zc[:, CONV_CH:])
        for c in range(N_SLAB):
            gl_ref[c, lo:hi, :] = glu[:, c * LANES:(c + 1) * LANES]

    def qkv_rows(r):
        z = jnp.dot(hb_ref[G_HALO + r:G_HALO + r + SB, :], wqkv_ref[...], preferred_element_type=F32)
        q_ref[r:r + SB, :] = (z[:, :ATTN_WIDTH] * SCALE).astype(BF16)
        kf = z[:, ATTN_WIDTH:ATTN_WIDTH + KV_WIDTH]
        vf = z[:, ATTN_WIDTH + KV_WIDTH:]
        k_ref[r:r + SB, :KV_WIDTH] = kf.astype(BF16)
        k_ref[r:r + SB, KV_WIDTH:] = pltpu.roll(kf, HEAD_DIM, axis=1).astype(BF16)
        v_ref[r:r + SB, :KV_WIDTH] = vf.astype(BF16)
        v_ref[r:r + SB, KV_WIDTH:] = pltpu.roll(vf, HEAD_DIM, axis=1).astype(BF16)

    def conv_rows(r):
        for c in range(N_SLAB):
            lanes = slice(c * LANES, (c + 1) * LANES)
            for rr in range(r, r + SB, CONV_ROWS):
                acc = jnp.broadcast_to(cb_ref[:, lanes], (CONV_ROWS, LANES))
                for t in range(CONV_K):
                    lo = rr + (G_HALO - CONV_PAD) + t
                    acc = acc + gl_ref[c, lo:lo + CONV_ROWS, :] * cw_ref[t:t + 1, lanes]
                cv_ref[rr:rr + CONV_ROWS, lanes] = acc
        cv = cv_ref[r:r + SB, :]
        mu = jnp.mean(cv, axis=-1, keepdims=True)
        xc = cv - mu
        ln = (xc * lax.rsqrt(jnp.mean(xc * xc, axis=-1, keepdims=True) + EPS)) * lng_ref[...] + lnb_ref[...]
        c_ref[r:r + SB, :] = (ln * _sigmoid(ln)).astype(BF16)

    n_sb = tm // SB
    bounds = [0] + [(k + 1) * SB + 2 * G_HALO for k in range(n_sb)]
    norm_rows(bounds[0], bounds[1])
    glu_rows(bounds[0], bounds[1])
    for sb in range(n_sb):
        if sb + 1 < n_sb:
            norm_rows(bounds[sb + 1], bounds[sb + 2])
            glu_rows(bounds[sb + 1], bounds[sb + 2])
        qkv_rows(sb * SB)
        conv_rows(sb * SB)


def _inproj(x, g, wqkv, wc, cw, cb, lng, lnb):
    b, s, _ = x.shape
    tm = TM_IN
    return pl.pallas_call(
        _inproj_kernel,
        grid=(b, s // tm),
        in_specs=_halo_specs(tm, G_HALO, s) + [
            _resident((1, D_MODEL)),
            _resident((D_MODEL, QKV_WIDTH)),
            _resident((D_MODEL, 2 * CONV_CH)),
            _resident((CONV_K, CONV_CH)),
            _resident((1, CONV_CH)),
            _resident((1, CONV_CH)),
            _resident((1, CONV_CH)),
        ],
        out_specs=[
            pl.BlockSpec((None, tm, ATTN_WIDTH), lambda bi, i: (bi, i, 0)),
            pl.BlockSpec((None, tm, 2 * KV_WIDTH), lambda bi, i: (bi, i, 0)),
            pl.BlockSpec((None, tm, 2 * KV_WIDTH), lambda bi, i: (bi, i, 0)),
            pl.BlockSpec((None, tm, CONV_CH), lambda bi, i: (bi, i, 0)),
        ],
        out_shape=[
            jax.ShapeDtypeStruct((b, s, ATTN_WIDTH), BF16),
            jax.ShapeDtypeStruct((b, s, 2 * KV_WIDTH), BF16),
            jax.ShapeDtypeStruct((b, s, 2 * KV_WIDTH), BF16),
            jax.ShapeDtypeStruct((b, s, CONV_CH), BF16),
        ],
        scratch_shapes=[
            pltpu.VMEM((tm + 2 * G_HALO, D_MODEL), BF16),
            pltpu.VMEM((N_SLAB, tm + 2 * G_HALO, LANES), F32),
            pltpu.VMEM((tm, CONV_CH), F32),
        ],
        compiler_params=pltpu.CompilerParams(
            dimension_semantics=("arbitrary", "arbitrary"), vmem_limit_bytes=VMEM_LIMIT_BYTES),
        name="inproj",
    )(x, x, x, g, wqkv, wc, cw, cb, lng, lnb)


def _mixer_kernel(sink_ref, x_ref, q_ref, k_ref, v_ref, c_ref, bias_ref, woa_ref, woc_ref, out_ref,
                  kz_ref, vz_ref, y_ref, sa_ref, ma_ref, sb_ref, mb_ref, *, seq):
    i = pl.program_id(1)
    r0 = i * TQ
    kv_rows = TQ + 2 * BLOCK
    n_blk = TQ // BLOCK

    ws = pl.multiple_of(jnp.clip(r0 - BLOCK, 0, seq - kv_rows), BLOCK)
    lo = lax.broadcasted_iota(jnp.int32, (1, LANES), 1) < HEAD_DIM
    zero = jnp.zeros((), BF16)
    for src_ref, dst_ref in ((k_ref, kz_ref), (v_ref, vz_ref)):
        w = src_ref[pl.ds(ws, kv_rows), :]
        straight, swapped = w[:, :KV_WIDTH], w[:, KV_WIDTH:]
        dst_ref[0] = jnp.where(lo, straight, zero)
        dst_ref[1] = jnp.where(lo, zero, swapped)
        dst_ref[2] = jnp.where(lo, swapped, zero)
        dst_ref[3] = jnp.where(lo, zero, straight)

    def block_pos(jb):
        r = r0 + jb * BLOCK
        koff = pl.multiple_of(jnp.clip(r - BLOCK, 0, seq - BAND) - ws, BLOCK)
        var = jnp.where(r == 0, 1, jnp.where(r == seq - BLOCK, 2, 0))
        return koff, var

    def score_stage(jb, s_ref, m_ref):
        koff, var = block_pos(jb)
        row = jb * BLOCK
        for n in range(N_HEADS):
            m, e = divmod(n, 2)
            h = m // (GROUP // 2)
            qp = q_ref[row:row + BLOCK, m * LANES:(m + 1) * LANES]
            kz = kz_ref[2 * h + e, pl.ds(koff, BAND), :]
            s = lax.dot_general(qp, kz, (((1,), (1,)), ((), ())), preferred_element_type=F32)
            s = s + bias_ref[var, n]
            s_ref[n] = s
            mx = jnp.maximum(jnp.max(s, axis=-1, keepdims=True), sink_ref[n])
            m_ref[n] = jnp.broadcast_to(mx, (BLOCK, LANES))

    def value_stage(jb, s_ref, m_ref):
        koff, _ = block_pos(jb)
        row = jb * BLOCK
        for m in range(N_HEADS // 2):
            h = m // (GROUP // 2)
            pv = []
            ls = []
            for e in range(2):
                n = 2 * m + e
                mx = m_ref[n]
                p = jnp.exp(s_ref[n] - jnp.concatenate([mx] * (BAND // LANES), axis=1))
                ls.append(jnp.sum(p, axis=-1, keepdims=True) + jnp.exp(sink_ref[n] - mx))
                vz = vz_ref[2 * h + e, pl.ds(koff, BAND), :]
                pv.append(jnp.dot(p.astype(BF16), vz, preferred_element_type=F32))
            denom = jnp.where(lo, ls[0], ls[1])
            y_ref[row:row + BLOCK, m * LANES:(m + 1) * LANES] = ((pv[0] + pv[1]) / denom).astype(BF16)

    def out_proj(jb):
        rows = slice(jb * BLOCK, (jb + 1) * BLOCK)
        out_ref[rows, :] = (x_ref[rows, :]
                            + jnp.dot(y_ref[rows, :], woa_ref[...], preferred_element_type=F32)
                            + jnp.dot(c_ref[rows, :], woc_ref[...], preferred_element_type=F32))

    stage = ((sa_ref, ma_ref), (sb_ref, mb_ref))
    score_stage(0, *stage[0])
    for jb in range(n_blk):
        if jb + 1 < n_blk:
            score_stage(jb + 1, *stage[(jb + 1) % 2])
        if jb > 0:
            out_proj(jb - 1)
        value_stage(jb, *stage[jb % 2])
    out_proj(n_blk - 1)


def _mixer(x, q, k, v, c_act, tables, sink, woa, woc):
    b, s, _ = x.shape
    kernel = functools.partial(_mixer_kernel, seq=s)
    return pl.pallas_call(
        kernel,
        grid=(b, s // TQ),
        in_specs=[
            pl.BlockSpec(memory_space=pltpu.SMEM),
            pl.BlockSpec((None, TQ, D_MODEL), lambda bi, i: (bi, i, 0)),
            pl.BlockSpec((None, TQ, ATTN_WIDTH), lambda bi, i: (bi, i, 0)),
            pl.BlockSpec((None, s, 2 * KV_WIDTH), lambda bi, i: (bi, 0, 0)),
            pl.BlockSpec((None, s, 2 * KV_WIDTH), lambda bi, i: (bi, 0, 0)),
            pl.BlockSpec((None, TQ, CONV_CH), lambda bi, i: (bi, i, 0)),
            _resident((N_VARIANT, N_HEADS, BLOCK, BAND)),
            _resident((ATTN_WIDTH, D_MODEL)),
            _resident((CONV_CH, D_MODEL)),
        ],
        out_specs=pl.BlockSpec((None, TQ, D_MODEL), lambda bi, i: (bi, i, 0)),
        out_shape=jax.ShapeDtypeStruct((b, s, D_MODEL), F32),
        scratch_shapes=[
            pltpu.VMEM((2 * N_KV, TQ + 2 * BLOCK, LANES), BF16),
            pltpu.VMEM((2 * N_KV, TQ + 2 * BLOCK, LANES), BF16),
            pltpu.VMEM((TQ, ATTN_WIDTH), BF16),
            pltpu.VMEM((N_HEADS, BLOCK, BAND), F32),
            pltpu.VMEM((N_HEADS, BLOCK, LANES), F32),
            pltpu.VMEM((N_HEADS, BLOCK, BAND), F32),
            pltpu.VMEM((N_HEADS, BLOCK, LANES), F32),
        ],
        compiler_params=pltpu.CompilerParams(
            dimension_semantics=("arbitrary", "arbitrary"), vmem_limit_bytes=VMEM_LIMIT_BYTES),
        name="mixer",
    )(sink, x, q, k, v, c_act, tables, woa, woc)


def _ffn_kernel(x_ref, xp_ref, xn_ref, g_ref, wup_ref, dw_ref, db_ref, wdn_ref, gf_ref, out_ref,
                hb_ref, uga_ref, uva_ref, ugb_ref, uvb_ref, *, final_norm):
    i = pl.program_id(1)
    first = i == 0
    last = i == pl.num_programs(1) - 1
    tm = TM_FF
    g = g_ref[...]
    hb_ref[:U_HALO, :] = jnp.where(first, 0.0, _rmsnorm(xp_ref[...], g)).astype(BF16)
    hb_ref[U_HALO:U_HALO + tm, :] = _rmsnorm(x_ref[...], g).astype(BF16)
    hb_ref[U_HALO + tm:, :] = jnp.where(last, 0.0, _rmsnorm(xn_ref[...], g)).astype(BF16)
    out_ref[...] = x_ref[...]

    def up_proj(j, ug_ref, uv_ref):
        hb = hb_ref[...]
        ug = jnp.dot(hb, wup_ref[0, j], preferred_element_type=F32)
        uv = jnp.dot(hb, wup_ref[1, j], preferred_element_type=F32)
        for c in range(CJ // LANES):
            ug_ref[c] = ug[:, c * LANES:(c + 1) * LANES]
            uv_ref[c] = uv[:, c * LANES:(c + 1) * LANES]

    def conv3(u_ref, c, w, bias):
        lanes = slice(c * LANES, (c + 1) * LANES)
        out = bias[:, lanes] + u_ref[c, U_HALO - 1:U_HALO - 1 + tm, :] * w[0:1, lanes]
        out = out + u_ref[c, U_HALO:U_HALO + tm, :] * w[1:2, lanes]
        return out + u_ref[c, U_HALO + 1:U_HALO + 1 + tm, :] * w[2:3, lanes]

    def down_proj(j, ug_ref, uv_ref):
        acts = []
        for c in range(CJ // LANES):
            gate = conv3(ug_ref, c, dw_ref[0, j], db_ref[0, j])
            val = conv3(uv_ref, c, dw_ref[1, j], db_ref[1, j])
            acts.append(((gate * _sigmoid(gate)) * val).astype(BF16))
        act = jnp.concatenate(acts, axis=1)
        out_ref[...] += jnp.dot(act, wdn_ref[j], preferred_element_type=F32)

    up_proj(0, uga_ref, uva_ref)

    def chunk_pair(p, carry):
        j = 2 * p
        up_proj(j + 1, ugb_ref, uvb_ref)
        down_proj(j, uga_ref, uva_ref)
        up_proj(j + 2, uga_ref, uva_ref)
        down_proj(j + 1, ugb_ref, uvb_ref)
        return carry

    lax.fori_loop(0, (NJ - 1) // 2, chunk_pair, 0)
    down_proj(NJ - 1, uga_ref, uva_ref)
    if final_norm:
        out_ref[...] = _rmsnorm(out_ref[...], gf_ref[...])


def _ffn(x, g, wup, dw, db, wdn, gf, final_norm):
    b, s, _ = x.shape
    tm = TM_FF
    kernel = functools.partial(_ffn_kernel, final_norm=final_norm)
    return pl.pallas_call(
        kernel,
        grid=(b, s // tm),
        in_specs=_halo_specs(tm, U_HALO, s) + [
            _resident((1, D_MODEL)),
            _resident((2, NJ, D_MODEL, CJ)),
            _resident((2, NJ, FFN_CONV_K, CJ)),
            _resident((2, NJ, 1, CJ)),
            _resident((NJ, CJ, D_MODEL)),
            _resident((1, D_MODEL)),
        ],
        out_specs=pl.BlockSpec((None, tm, D_MODEL), lambda bi, i: (bi, i, 0)),
        out_shape=jax.ShapeDtypeStruct((b, s, D_MODEL), F32),
        scratch_shapes=[pltpu.VMEM((tm + 2 * U_HALO, D_MODEL), BF16)]
        + [pltpu.VMEM((CJ // LANES, tm + 2 * U_HALO, LANES), F32)] * 4,
        compiler_params=pltpu.CompilerParams(
            dimension_semantics=("arbitrary", "arbitrary"), vmem_limit_bytes=VMEM_LIMIT_BYTES),
        name="ffn",
    )(x, x, x, g, wup, dw, db, wdn, gf)


def _split_ff(a):
    lead = a.shape[:-1]
    a = a.reshape(lead + (2, NJ, CJ))
    return jnp.moveaxis(a, (-3, -2), (0, 1))


def _prepare_layer(l, norm_attn_g, w_in, attn_sink, conv_dw_w, conv_dw_b, conv_ln_g, conv_ln_b, w_out,
                   norm_ffn_g, w_up, ffn_dw_w, ffn_dw_b, w_down):
    return dict(
        g_attn=norm_attn_g[l].astype(F32)[None, :],
        wqkv=w_in[l][:, :QKV_WIDTH].astype(BF16),
        wc=w_in[l][:, QKV_WIDTH:].astype(BF16),
        sink=attn_sink[l].astype(F32),
        cw=conv_dw_w[l].astype(F32),
        cb=conv_dw_b[l].astype(F32)[None, :],
        lng=conv_ln_g[l].astype(F32)[None, :],
        lnb=conv_ln_b[l].astype(F32)[None, :],
        woa=w_out[l][:ATTN_WIDTH].astype(BF16),
        woc=w_out[l][ATTN_WIDTH:].astype(BF16),
        g_ffn=norm_ffn_g[l].astype(F32)[None, :],
        wup=_split_ff(w_up[l].astype(BF16)),
        dw=_split_ff(ffn_dw_w[l].astype(F32)),
        db=_split_ff(ffn_dw_b[l].astype(F32)[None, :]),
        wdn=w_down[l].astype(BF16).reshape(NJ, CJ, D_MODEL),
    )


def _trunk(x, tables, layers, gf):
    depth = len(layers)
    for l, p in enumerate(layers):
        q, k, v, c_act = _inproj(x, p["g_attn"], p["wqkv"], p["wc"], p["cw"], p["cb"], p["lng"], p["lnb"])
        x = _mixer(x, q, k, v, c_act, tables, p["sink"], p["woa"], p["woc"])
        x = _ffn(x, p["g_ffn"], p["wup"], p["dw"], p["db"], p["wdn"], gf, final_norm=(l == depth - 1))
    return x


def kernel(x_prompt, x_sample, rel_bias, norm_attn_g, w_in, attn_sink, conv_dw_w, conv_dw_b, conv_ln_g,
           conv_ln_b, w_out, norm_ffn_g, w_up, ffn_dw_w, ffn_dw_b, w_down, norm_final_g):
    depth = w_in.shape[0]
    tables = _band_tables(rel_bias)
    layers = [
        _prepare_layer(l, norm_attn_g, w_in, attn_sink, conv_dw_w, conv_dw_b, conv_ln_g, conv_ln_b, w_out,
                       norm_ffn_g, w_up, ffn_dw_w, ffn_dw_b, w_down)
        for l in range(depth)
    ]
    gf = norm_final_g.astype(F32)[None, :]
    return (_trunk(x_prompt, tables, layers, gf), _trunk(x_sample, tables, layers, gf))
```

```python
import functools
import math

import jax
import jax.numpy as jnp
from jax import lax
from jax.experimental import pallas as pl
from jax.experimental.pallas import tpu as pltpu

F32 = jnp.float32
BF16 = jnp.bfloat16

D_MODEL = 1024
HEAD_DIM = 64
N_HEADS = 8
N_KV = 2
GROUP = N_HEADS // N_KV
ATTN_WIDTH = N_HEADS * HEAD_DIM
KV_WIDTH = N_KV * HEAD_DIM
CONV_CH = D_MODEL - ATTN_WIDTH
CONV_K = 31
CONV_PAD = (CONV_K - 1) // 2
WINDOW = 128
BLOCK = 128
BAND = 3 * BLOCK
N_BUCKETS = 32
MAX_DISTANCE = 128
D_FF = 2816
FFN_CONV_K = 3
QKV_WIDTH = ATTN_WIDTH + 2 * KV_WIDTH
EPS = 1e-6
NEG = -1e30
SCALE = HEAD_DIM ** -0.5
LOG2E = math.log2(math.e)

LANES = 128
SUBLANES = 8
VMEM_LIMIT_BYTES = 56 * 1024 * 1024

TM_IN = 1024
TQ = 1024
TM_FF = 1024
CJ = 256
NJ = D_FF // CJ
N_SLAB = CONV_CH // LANES
SB = 128
CONV_ROWS = 32
G_HALO = 16
U_HALO = SUBLANES
N_VARIANT = 3


def _sigmoid(x):
    return 1.0 / (1.0 + jnp.exp(-x))


def _rmsnorm(x, g):
    ms = jnp.mean(x * x, axis=-1, keepdims=True)
    return (x * lax.rsqrt(ms + EPS)) * g


def _resident(shape):
    return pl.BlockSpec(shape, lambda bi, i: (0,) * len(shape), pipeline_mode=pl.Buffered(1))


def _halo_specs(tm, halo, seq):
    per_tile = tm // halo
    n_halo = seq // halo
    return [
        pl.BlockSpec((None, tm, D_MODEL), lambda bi, i: (bi, i, 0)),
        pl.BlockSpec((None, halo, D_MODEL), lambda bi, i: (bi, jnp.maximum(i * per_tile - 1, 0), 0)),
        pl.BlockSpec((None, halo, D_MODEL), lambda bi, i: (bi, jnp.minimum((i + 1) * per_tile, n_halo - 1), 0)),
    ]


def _bias_table_kernel(rb_ref, bucket_ref, valid_ref, out_ref):
    n = pl.program_id(1)
    bucket = bucket_ref[...]
    val = jnp.zeros(bucket.shape, F32)
    for b in range(N_BUCKETS):
        val = jnp.where(bucket == b, rb_ref[b, n], val)
    out_ref[...] = jnp.where(valid_ref[...] != 0, val * LOG2E, NEG)


def _t5_buckets(rel):
    nb = N_BUCKETS // 2
    ret = (rel > 0).astype(jnp.int32) * nb
    n = jnp.abs(rel)
    max_exact = nb // 2
    nf = jnp.maximum(n, 1).astype(F32)
    large = max_exact + (jnp.log(nf / max_exact) / math.log(MAX_DISTANCE / max_exact)
                         * (nb - max_exact)).astype(jnp.int32)
    large = jnp.minimum(large, nb - 1)
    return ret + jnp.where(n < max_exact, n, large)


def _band_tables(rel_bias):
    qi = jnp.arange(BLOCK, dtype=jnp.int32)[:, None]
    kj = jnp.arange(BAND, dtype=jnp.int32)[None, :]
    offsets = jnp.array([-BLOCK, 0, -2 * BLOCK], jnp.int32)[:, None, None]
    rel = (kj - qi)[None] + offsets
    bucket = _t5_buckets(rel)
    valid = (jnp.abs(rel) <= WINDOW).astype(jnp.int32)
    return pl.pallas_call(
        _bias_table_kernel,
        grid=(N_VARIANT, N_HEADS),
        in_specs=[
            pl.BlockSpec(memory_space=pltpu.SMEM),
            pl.BlockSpec((None, BLOCK, BAND), lambda v, n: (v, 0, 0)),
            pl.BlockSpec((None, BLOCK, BAND), lambda v, n: (v, 0, 0)),
        ],
        out_specs=pl.BlockSpec((None, None, BLOCK, BAND), lambda v, n: (v, n, 0, 0)),
        out_shape=jax.ShapeDtypeStruct((N_VARIANT, N_HEADS, BLOCK, BAND), F32),
        name="band_bias_tables",
    )(rel_bias.astype(F32), bucket, valid)


def _inproj_kernel(x_ref, xp_ref, xn_ref, g_ref, wqkv_ref, wc_ref, cw_ref, cb_ref, lng_ref, lnb_ref,
                   q_ref, k_ref, v_ref, c_ref, hb_ref, gl_ref, cv_ref):
    i = pl.program_id(1)
    first = i == 0
    last = i == pl.num_programs(1) - 1
    tm = TM_IN
    g = g_ref[...]

    def norm_rows(lo, hi):
        if lo < G_HALO:
            hb_ref[lo:G_HALO, :] = jnp.where(first, 0.0, _rmsnorm(xp_ref[lo:, :], g)).astype(BF16)
        m_lo, m_hi = max(lo, G_HALO), min(hi, G_HALO + tm)
        hb_ref[m_lo:m_hi, :] = _rmsnorm(x_ref[m_lo - G_HALO:m_hi - G_HALO, :], g).astype(BF16)
        if hi > G_HALO + tm:
            hb_ref[G_HALO + tm:hi, :] = jnp.where(last, 0.0, _rmsnorm(xn_ref[:hi - G_HALO - tm, :], g)).astype(BF16)

    def glu_rows(lo, hi):
        zc = jnp.dot(hb_ref[lo:hi, :], wc_ref[...], preferred_element_type=F32)
        glu = zc[:, :CONV_CH] * _sigmoid(zc[:, CONV_CH:])
        for c in range(N_SLAB):
            gl_ref[c, lo:hi, :] = glu[:, c * LANES:(c + 1) * LANES]

    def qkv_rows(r):
        z = jnp.dot(hb_ref[G_HALO + r:G_HALO + r + SB, :], wqkv_ref[...], preferred_element_type=F32)
        q_ref[r:r + SB, :] = (z[:, :ATTN_WIDTH] * (SCALE * LOG2E)).astype(BF16)
        kf = z[:, ATTN_WIDTH:ATTN_WIDTH + KV_WIDTH]
        vf = z[:, ATTN_WIDTH + KV_WIDTH:]
        k_ref[r:r + SB, :KV_WIDTH] = kf.astype(BF16)
        k_ref[r:r + SB, KV_WIDTH:] = pltpu.roll(kf, HEAD_DIM, axis=1).astype(BF16)
        v_ref[r:r + SB, :KV_WIDTH] = vf.astype(BF16)
        v_ref[r:r + SB, KV_WIDTH:] = pltpu.roll(vf, HEAD_DIM, axis=1).astype(BF16)

    def conv_rows(r):
        for c in range(N_SLAB):
            lanes = slice(c * LANES, (c + 1) * LANES)
            for rr in range(r, r + SB, CONV_ROWS):
                acc = jnp.broadcast_to(cb_ref[:, lanes], (CONV_ROWS, LANES))
                for t in range(CONV_K):
                    lo = rr + (G_HALO - CONV_PAD) + t
                    acc = acc + gl_ref[c, lo:lo + CONV_ROWS, :] * cw_ref[t:t + 1, lanes]
                cv_ref[rr:rr + CONV_ROWS, lanes] = acc
        cv = cv_ref[r:r + SB, :]
        mu = jnp.mean(cv, axis=-1, keepdims=True)
        xc = cv - mu
        ln = (xc * lax.rsqrt(jnp.mean(xc * xc, axis=-1, keepdims=True) + EPS)) * lng_ref[...] + lnb_ref[...]
        c_ref[r:r + SB, :] = (ln * _sigmoid(ln)).astype(BF16)

    n_sb = tm // SB
    bounds = [0] + [(k + 1) * SB + 2 * G_HALO for k in range(n_sb)]
    norm_rows(bounds[0], bounds[1])
    glu_rows(bounds[0], bounds[1])
    for sb in range(n_sb):
        if sb + 1 < n_sb:
            norm_rows(bounds[sb + 1], bounds[sb + 2])
            glu_rows(bounds[sb + 1], bounds[sb + 2])
        qkv_rows(sb * SB)
        conv_rows(sb * SB)


def _inproj(x, g, wqkv, wc, cw, cb, lng, lnb):
    b, s, _ = x.shape
    tm = TM_IN
    return pl.pallas_call(
        _inproj_kernel,
        grid=(b, s // tm),
        in_specs=_halo_specs(tm, G_HALO, s) + [
            _resident((1, D_MODEL)),
            _resident((D_MODEL, QKV_WIDTH)),
            _resident((D_MODEL, 2 * CONV_CH)),
            _resident((CONV_K, CONV_CH)),
            _resident((1, CONV_CH)),
            _resident((1, CONV_CH)),
            _resident((1, CONV_CH)),
        ],
        out_specs=[
            pl.BlockSpec((None, tm, ATTN_WIDTH), lambda bi, i: (bi, i, 0)),
            pl.BlockSpec((None, tm, 2 * KV_WIDTH), lambda bi, i: (bi, i, 0)),
            pl.BlockSpec((None, tm, 2 * KV_WIDTH), lambda bi, i: (bi, i, 0)),
            pl.BlockSpec((None, tm, CONV_CH), lambda bi, i: (bi, i, 0)),
        ],
        out_shape=[
            jax.ShapeDtypeStruct((b, s, ATTN_WIDTH), BF16),
            jax.ShapeDtypeStruct((b, s, 2 * KV_WIDTH), BF16),
            jax.ShapeDtypeStruct((b, s, 2 * KV_WIDTH), BF16),
            jax.ShapeDtypeStruct((b, s, CONV_CH), BF16),
        ],
        scratch_shapes=[
            pltpu.VMEM((tm + 2 * G_HALO, D_MODEL), BF16),
            pltpu.VMEM((N_SLAB, tm + 2 * G_HALO, LANES), F32),
            pltpu.VMEM((tm, CONV_CH), F32),
        ],
        compiler_params=pltpu.CompilerParams(
            dimension_semantics=("arbitrary", "arbitrary"), vmem_limit_bytes=VMEM_LIMIT_BYTES),
        name="inproj",
    )(x, x, x, g, wqkv, wc, cw, cb, lng, lnb)


def _mixer_kernel(sink_ref, x_ref, q_ref, k_ref, v_ref, c_ref, bias_ref, woa_ref, woc_ref, out_ref,
                  kz_ref, vz_ref, y_ref, sa_ref, ma_ref, sb_ref, mb_ref, *, seq):
    i = pl.program_id(1)
    r0 = i * TQ
    kv_rows = TQ + 2 * BLOCK
    n_blk = TQ // BLOCK

    ws = pl.multiple_of(jnp.clip(r0 - BLOCK, 0, seq - kv_rows), BLOCK)
    lo = lax.broadcasted_iota(jnp.int32, (1, LANES), 1) < HEAD_DIM
    zero = jnp.zeros((), BF16)
    for src_ref, dst_ref in ((k_ref, kz_ref), (v_ref, vz_ref)):
        w = src_ref[pl.ds(ws, kv_rows), :]
        straight, swapped = w[:, :KV_WIDTH], w[:, KV_WIDTH:]
        dst_ref[0] = jnp.where(lo, straight, zero)
        dst_ref[1] = jnp.where(lo, zero, swapped)
        dst_ref[2] = jnp.where(lo, swapped, zero)
        dst_ref[3] = jnp.where(lo, zero, straight)

    def block_pos(jb):
        r = r0 + jb * BLOCK
        koff = pl.multiple_of(jnp.clip(r - BLOCK, 0, seq - BAND) - ws, BLOCK)
        var = jnp.where(r == 0, 1, jnp.where(r == seq - BLOCK, 2, 0))
        return koff, var

    def score_stage(jb, s_ref, m_ref):
        koff, var = block_pos(jb)
        row = jb * BLOCK
        for n in range(N_HEADS):
            m, e = divmod(n, 2)
            h = m // (GROUP // 2)
            qp = q_ref[row:row + BLOCK, m * LANES:(m + 1) * LANES]
            kz = kz_ref[2 * h + e, pl.ds(koff, BAND), :]
            s = lax.dot_general(qp, kz, (((1,), (1,)), ((), ())), preferred_element_type=F32)
            s = s + bias_ref[var, n]
            s_ref[n] = s
            mx = jnp.maximum(jnp.max(s, axis=-1, keepdims=True), sink_ref[n] * LOG2E)
            m_ref[n] = jnp.broadcast_to(mx, (BLOCK, LANES))

    def value_stage(jb, s_ref, m_ref):
        koff, _ = block_pos(jb)
        row = jb * BLOCK
        for m in range(N_HEADS // 2):
            h = m // (GROUP // 2)
            pv = []
            ls = []
            for e in range(2):
                n = 2 * m + e
                mx = m_ref[n]
                p = jnp.exp2(s_ref[n] - jnp.concatenate([mx] * (BAND // LANES), axis=1))
                ls.append(jnp.sum(p, axis=-1, keepdims=True) + jnp.exp2(sink_ref[n] * LOG2E - mx))
                vz = vz_ref[2 * h + e, pl.ds(koff, BAND), :]
                pv.append(jnp.dot(p.astype(BF16), vz, preferred_element_type=F32))
            denom = jnp.where(lo, ls[0], ls[1])
            y_ref[row:row + BLOCK, m * LANES:(m + 1) * LANES] = ((pv[0] + pv[1]) / denom).astype(BF16)

    def out_proj(jb):
        rows = slice(jb * BLOCK, (jb + 1) * BLOCK)
        out_ref[rows, :] = (x_ref[rows, :]
                            + jnp.dot(y_ref[rows, :], woa_ref[...], preferred_element_type=F32)
                            + jnp.dot(c_ref[rows, :], woc_ref[...], preferred_element_type=F32))

    stage = ((sa_ref, ma_ref), (sb_ref, mb_ref))
    score_stage(0, *stage[0])
    for jb in range(n_blk):
        if jb + 1 < n_blk:
            score_stage(jb + 1, *stage[(jb + 1) % 2])
        if jb > 0:
            out_proj(jb - 1)
        value_stage(jb, *stage[jb % 2])
    out_proj(n_blk - 1)


def _mixer(x, q, k, v, c_act, tables, sink, woa, woc):
    b, s, _ = x.shape
    kernel = functools.partial(_mixer_kernel, seq=s)
    return pl.pallas_call(
        kernel,
        grid=(b, s // TQ),
        in_specs=[
            pl.BlockSpec(memory_space=pltpu.SMEM),
            pl.BlockSpec((None, TQ, D_MODEL), lambda bi, i: (bi, i, 0)),
            pl.BlockSpec((None, TQ, ATTN_WIDTH), lambda bi, i: (bi, i, 0)),
            pl.BlockSpec((None, s, 2 * KV_WIDTH), lambda bi, i: (bi, 0, 0), pipeline_mode=pl.Buffered(1)),
            pl.BlockSpec((None, s, 2 * KV_WIDTH), lambda bi, i: (bi, 0, 0), pipeline_mode=pl.Buffered(1)),
            pl.BlockSpec((None, TQ, CONV_CH), lambda bi, i: (bi, i, 0)),
            _resident((N_VARIANT, N_HEADS, BLOCK, BAND)),
            _resident((ATTN_WIDTH, D_MODEL)),
            _resident((CONV_CH, D_MODEL)),
        ],
        out_specs=pl.BlockSpec((None, TQ, D_MODEL), lambda bi, i: (bi, i, 0)),
        out_shape=jax.ShapeDtypeStruct((b, s, D_MODEL), F32),
        scratch_shapes=[
            pltpu.VMEM((2 * N_KV, TQ + 2 * BLOCK, LANES), BF16),
            pltpu.VMEM((2 * N_KV, TQ + 2 * BLOCK, LANES), BF16),
            pltpu.VMEM((TQ, ATTN_WIDTH), BF16),
            pltpu.VMEM((N_HEADS, BLOCK, BAND), F32),
            pltpu.VMEM((N_HEADS, BLOCK, LANES), F32),
            pltpu.VMEM((N_HEADS, BLOCK, BAND), F32),
            pltpu.VMEM((N_HEADS, BLOCK, LANES), F32),
        ],
        compiler_params=pltpu.CompilerParams(
            dimension_semantics=("arbitrary", "arbitrary"), vmem_limit_bytes=VMEM_LIMIT_BYTES),
        name="mixer",
    )(sink, x, q, k, v, c_act, tables, woa, woc)


def _ffn_kernel(x_ref, xp_ref, xn_ref, g_ref, wup_ref, dw_ref, db_ref, wdn_ref, gf_ref, out_ref,
                hb_ref, uga_ref, uva_ref, ugb_ref, uvb_ref, *, final_norm):
    i = pl.program_id(1)
    first = i == 0
    last = i == pl.num_programs(1) - 1
    tm = TM_FF
    g = g_ref[...]
    hb_ref[:U_HALO, :] = jnp.where(first, 0.0, _rmsnorm(xp_ref[...], g)).astype(BF16)
    hb_ref[U_HALO:U_HALO + tm, :] = _rmsnorm(x_ref[...], g).astype(BF16)
    hb_ref[U_HALO + tm:, :] = jnp.where(last, 0.0, _rmsnorm(xn_ref[...], g)).astype(BF16)
    out_ref[...] = x_ref[...]

    def up_proj(j, ug_ref, uv_ref):
        hb = hb_ref[...]
        ug = jnp.dot(hb, wup_ref[0, j], preferred_element_type=F32)
        uv = jnp.dot(hb, wup_ref[1, j], preferred_element_type=F32)
        for c in range(CJ // LANES):
            ug_ref[c] = ug[:, c * LANES:(c + 1) * LANES]
            uv_ref[c] = uv[:, c * LANES:(c + 1) * LANES]

    def conv3(u_ref, c, w, bias):
        lanes = slice(c * LANES, (c + 1) * LANES)
        out = bias[:, lanes] + u_ref[c, U_HALO - 1:U_HALO - 1 + tm, :] * w[0:1, lanes]
        out = out + u_ref[c, U_HALO:U_HALO + tm, :] * w[1:2, lanes]
        return out + u_ref[c, U_HALO + 1:U_HALO + 1 + tm, :] * w[2:3, lanes]

    def down_proj(j, ug_ref, uv_ref):
        acts = []
        for c in range(CJ // LANES):
            gate = conv3(ug_ref, c, dw_ref[0, j], db_ref[0, j])
            val = conv3(uv_ref, c, dw_ref[1, j], db_ref[1, j])
            acts.append(((gate * _sigmoid(gate)) * val).astype(BF16))
        act = jnp.concatenate(acts, axis=1)
        out_ref[...] += jnp.dot(act, wdn_ref[j], preferred_element_type=F32)

    up_proj(0, uga_ref, uva_ref)

    def chunk_pair(j):
        up_proj(j + 1, ugb_ref, uvb_ref)
        down_proj(j, uga_ref, uva_ref)
        up_proj(j + 2, uga_ref, uva_ref)
        down_proj(j + 1, ugb_ref, uvb_ref)

    def chunk_quad(p, carry):
        chunk_pair(4 * p)
        chunk_pair(4 * p + 2)
        return carry

    n_quad = (NJ - 1) // 4
    lax.fori_loop(0, n_quad, chunk_quad, 0)
    for j in range(4 * n_quad, NJ - 1, 2):
        chunk_pair(j)
    down_proj(NJ - 1, uga_ref, uva_ref)
    if final_norm:
        out_ref[...] = _rmsnorm(out_ref[...], gf_ref[...])


def _ffn(x, g, wup, dw, db, wdn, gf, final_norm):
    b, s, _ = x.shape
    tm = TM_FF
    kernel = functools.partial(_ffn_kernel, final_norm=final_norm)
    return pl.pallas_call(
        kernel,
        grid=(b, s // tm),
        in_specs=_halo_specs(tm, U_HALO, s) + [
            _resident((1, D_MODEL)),
            _resident((2, NJ, D_MODEL, CJ)),
            _resident((2, NJ, FFN_CONV_K, CJ)),
            _resident((2, NJ, 1, CJ)),
            _resident((NJ, CJ, D_MODEL)),
            _resident((1, D_MODEL)),
        ],
        out_specs=pl.BlockSpec((None, tm, D_MODEL), lambda bi, i: (bi, i, 0)),
        out_shape=jax.ShapeDtypeStruct((b, s, D_MODEL), F32),
        scratch_shapes=[pltpu.VMEM((tm + 2 * U_HALO, D_MODEL), BF16)]
        + [pltpu.VMEM((CJ // LANES, tm + 2 * U_HALO, LANES), F32)] * 4,
        compiler_params=pltpu.CompilerParams(
            dimension_semantics=("arbitrary", "arbitrary"), vmem_limit_bytes=VMEM_LIMIT_BYTES),
        name="ffn",
    )(x, x, x, g, wup, dw, db, wdn, gf)


def _split_ff(a):
    lead = a.shape[:-1]
    a = a.reshape(lead + (2, NJ, CJ))
    return jnp.moveaxis(a, (-3, -2), (0, 1))


def _prepare_layer(l, norm_attn_g, w_in, attn_sink, conv_dw_w, conv_dw_b, conv_ln_g, conv_ln_b, w_out,
                   norm_ffn_g, w_up, ffn_dw_w, ffn_dw_b, w_down):
    return dict(
        g_attn=norm_attn_g[l].astype(F32)[None, :],
        wqkv=w_in[l][:, :QKV_WIDTH].astype(BF16),
        wc=w_in[l][:, QKV_WIDTH:].astype(BF16),
        sink=attn_sink[l].astype(F32),
        cw=conv_dw_w[l].astype(F32),
        cb=conv_dw_b[l].astype(F32)[None, :],
        lng=conv_ln_g[l].astype(F32)[None, :],
        lnb=conv_ln_b[l].astype(F32)[None, :],
        woa=w_out[l][:ATTN_WIDTH].astype(BF16),
        woc=w_out[l][ATTN_WIDTH:].astype(BF16),
        g_ffn=norm_ffn_g[l].astype(F32)[None, :],
        wup=_split_ff(w_up[l].astype(BF16)),
        dw=_split_ff(ffn_dw_w[l].astype(F32)),
        db=_split_ff(ffn_dw_b[l].astype(F32)[None, :]),
        wdn=w_down[l].astype(BF16).reshape(NJ, CJ, D_MODEL),
    )


def _trunk(x, tables, layers, gf):
    depth = len(layers)
    for l, p in enumerate(layers):
        q, k, v, c_act = _inproj(x, p["g_attn"], p["wqkv"], p["wc"], p["cw"], p["cb"], p["lng"], p["lnb"])
        x = _mixer(x, q, k, v, c_act, tables, p["sink"], p["woa"], p["woc"])
        x = _ffn(x, p["g_ffn"], p["wup"], p["dw"], p["db"], p["wdn"], gf, final_norm=(l == depth - 1))
    return x


def kernel(x_prompt, x_sample, rel_bias, norm_attn_g, w_in, attn_sink, conv_dw_w, conv_dw_b, conv_ln_g,
           conv_ln_b, w_out, norm_ffn_g, w_up, ffn_dw_w, ffn_dw_b, w_down, norm_final_g):
    depth = w_in.shape[0]
    tables = _band_tables(rel_bias)
    layers = [
        _prepare_layer(l, norm_attn_g, w_in, attn_sink, conv_dw_w, conv_dw_b, conv_ln_g, conv_ln_b, w_out,
                       norm_ffn_g, w_up, ffn_dw_w, ffn_dw_b, w_down)
        for l in range(depth)
    ]
    gf = norm_final_g.astype(F32)[None, :]
    return (_trunk(x_prompt, tables, layers, gf), _trunk(x_sample, tables, layers, gf))
```

```python
import functools
import math

import jax
import jax.numpy as jnp
from jax import lax
from jax.experimental import pallas as pl
from jax.experimental.pallas import tpu as pltpu

F32 = jnp.float32
BF16 = jnp.bfloat16

D_MODEL = 1024
HEAD_DIM = 64
N_HEADS = 8
N_KV = 2
GROUP = N_HEADS // N_KV
ATTN_WIDTH = N_HEADS * HEAD_DIM
KV_WIDTH = N_KV * HEAD_DIM
CONV_CH = D_MODEL - ATTN_WIDTH
CONV_K = 31
CONV_PAD = (CONV_K - 1) // 2
WINDOW = 128
BLOCK = 128
BAND = 3 * BLOCK
N_BUCKETS = 32
MAX_DISTANCE = 128
D_FF = 2816
FFN_CONV_K = 3
QKV_WIDTH = ATTN_WIDTH + 2 * KV_WIDTH
EPS = 1e-6
NEG = -1e30
SCALE = HEAD_DIM ** -0.5
LOG2E = math.log2(math.e)

LANES = 128
SUBLANES = 8
VMEM_LIMIT_BYTES = 56 * 1024 * 1024

TM_IN = 1024
TQ = 1024
TM_FF = 1024
CJ = 256
NJ = D_FF // CJ
N_SLAB = CONV_CH // LANES
SB = 128
CONV_ROWS = 32
G_HALO = 16
U_HALO = SUBLANES
N_VARIANT = 3


def _sigmoid(x):
    return 1.0 / (1.0 + jnp.exp2(x * -LOG2E))


def _rmsnorm(x, g):
    ms = jnp.mean(x * x, axis=-1, keepdims=True)
    return (x * lax.rsqrt(ms + EPS)) * g


def _resident(shape):
    return pl.BlockSpec(shape, lambda bi, i: (0,) * len(shape), pipeline_mode=pl.Buffered(1))


def _halo_specs(tm, halo, seq):
    per_tile = tm // halo
    n_halo = seq // halo
    return [
        pl.BlockSpec((None, tm, D_MODEL), lambda bi, i: (bi, i, 0)),
        pl.BlockSpec((None, halo, D_MODEL), lambda bi, i: (bi, jnp.maximum(i * per_tile - 1, 0), 0)),
        pl.BlockSpec((None, halo, D_MODEL), lambda bi, i: (bi, jnp.minimum((i + 1) * per_tile, n_halo - 1), 0)),
    ]


def _bias_table_kernel(rb_ref, bucket_ref, valid_ref, out_ref):
    bucket = bucket_ref[...]
    valid = valid_ref[...] != 0
    for n in range(N_HEADS):
        val = jnp.zeros(bucket.shape, F32)
        for b in range(N_BUCKETS):
            val = jnp.where(bucket == b, rb_ref[b, n], val)
        out_ref[n] = jnp.where(valid, val * LOG2E, NEG)


def _t5_buckets(rel):
    nb = N_BUCKETS // 2
    ret = (rel > 0).astype(jnp.int32) * nb
    n = jnp.abs(rel)
    max_exact = nb // 2
    nf = jnp.maximum(n, 1).astype(F32)
    large = max_exact + (jnp.log(nf / max_exact) / math.log(MAX_DISTANCE / max_exact)
                         * (nb - max_exact)).astype(jnp.int32)
    large = jnp.minimum(large, nb - 1)
    return ret + jnp.where(n < max_exact, n, large)


def _band_tables(rel_bias):
    qi = jnp.arange(BLOCK, dtype=jnp.int32)[:, None]
    kj = jnp.arange(BAND, dtype=jnp.int32)[None, :]
    offsets = jnp.array([-BLOCK, 0, -2 * BLOCK], jnp.int32)[:, None, None]
    rel = (kj - qi)[None] + offsets
    bucket = _t5_buckets(rel)
    valid = (jnp.abs(rel) <= WINDOW).astype(jnp.int32)
    return pl.pallas_call(
        _bias_table_kernel,
        grid=(N_VARIANT,),
        in_specs=[
            pl.BlockSpec(memory_space=pltpu.SMEM),
            pl.BlockSpec((None, BLOCK, BAND), lambda v: (v, 0, 0)),
            pl.BlockSpec((None, BLOCK, BAND), lambda v: (v, 0, 0)),
        ],
        out_specs=pl.BlockSpec((None, N_HEADS, BLOCK, BAND), lambda v: (v, 0, 0, 0)),
        out_shape=jax.ShapeDtypeStruct((N_VARIANT, N_HEADS, BLOCK, BAND), F32),
        name="band_bias_tables",
    )(rel_bias.astype(F32), bucket, valid)


def _inproj_kernel(x_ref, xp_ref, xn_ref, g_ref, wqkv_ref, wc_ref, cw_ref, cb_ref, lng_ref, lnb_ref,
                   q_ref, k_ref, v_ref, c_ref, hb_ref, gl_ref, cv_ref):
    i = pl.program_id(1)
    first = i == 0
    last = i == pl.num_programs(1) - 1
    tm = TM_IN
    g = g_ref[...]

    def norm_rows(lo, hi):
        if lo < G_HALO:
            hb_ref[lo:G_HALO, :] = jnp.where(first, 0.0, _rmsnorm(xp_ref[lo:, :], g)).astype(BF16)
        m_lo, m_hi = max(lo, G_HALO), min(hi, G_HALO + tm)
        hb_ref[m_lo:m_hi, :] = _rmsnorm(x_ref[m_lo - G_HALO:m_hi - G_HALO, :], g).astype(BF16)
        if hi > G_HALO + tm:
            hb_ref[G_HALO + tm:hi, :] = jnp.where(last, 0.0, _rmsnorm(xn_ref[:hi - G_HALO - tm, :], g)).astype(BF16)

    def glu_rows(lo, hi):
        zc = jnp.dot(hb_ref[lo:hi, :], wc_ref[...], preferred_element_type=F32)
        glu = zc[:, :CONV_CH] * _sigmoid(zc[:, CONV_CH:])
        for c in range(N_SLAB):
            gl_ref[c, lo:hi, :] = glu[:, c * LANES:(c + 1) * LANES]

    def qkv_rows(r):
        z = jnp.dot(hb_ref[G_HALO + r:G_HALO + r + SB, :], wqkv_ref[...], preferred_element_type=F32)
        q_ref[r:r + SB, :] = (z[:, :ATTN_WIDTH] * (SCALE * LOG2E)).astype(BF16)
        kf = z[:, ATTN_WIDTH:ATTN_WIDTH + KV_WIDTH]
        vf = z[:, ATTN_WIDTH + KV_WIDTH:]
        k_ref[r:r + SB, :KV_WIDTH] = kf.astype(BF16)
        k_ref[r:r + SB, KV_WIDTH:] = pltpu.roll(kf, HEAD_DIM, axis=1).astype(BF16)
        v_ref[r:r + SB, :KV_WIDTH] = vf.astype(BF16)
        v_ref[r:r + SB, KV_WIDTH:] = pltpu.roll(vf, HEAD_DIM, axis=1).astype(BF16)

    def conv_rows(r):
        for c in range(N_SLAB):
            lanes = slice(c * LANES, (c + 1) * LANES)
            for rr in range(r, r + SB, CONV_ROWS):
                acc = jnp.broadcast_to(cb_ref[:, lanes], (CONV_ROWS, LANES))
                for t in range(CONV_K):
                    lo = rr + (G_HALO - CONV_PAD) + t
                    acc = acc + gl_ref[c, lo:lo + CONV_ROWS, :] * cw_ref[t:t + 1, lanes]
                cv_ref[rr:rr + CONV_ROWS, lanes] = acc
        cv = cv_ref[r:r + SB, :]
        mu = jnp.mean(cv, axis=-1, keepdims=True)
        xc = cv - mu
        ln = (xc * lax.rsqrt(jnp.mean(xc * xc, axis=-1, keepdims=True) + EPS)) * lng_ref[...] + lnb_ref[...]
        c_ref[r:r + SB, :] = (ln * _sigmoid(ln)).astype(BF16)

    n_sb = tm // SB
    bounds = [0] + [(k + 1) * SB + 2 * G_HALO for k in range(n_sb)]
    norm_rows(bounds[0], bounds[1])
    glu_rows(bounds[0], bounds[1])
    for sb in range(n_sb):
        if sb + 1 < n_sb:
            norm_rows(bounds[sb + 1], bounds[sb + 2])
            glu_rows(bounds[sb + 1], bounds[sb + 2])
        qkv_rows(sb * SB)
        conv_rows(sb * SB)


def _inproj(x, g, wqkv, wc, cw, cb, lng, lnb):
    b, s, _ = x.shape
    tm = TM_IN
    return pl.pallas_call(
        _inproj_kernel,
        grid=(b, s // tm),
        in_specs=_halo_specs(tm, G_HALO, s) + [
            _resident((1, D_MODEL)),
            _resident((D_MODEL, QKV_WIDTH)),
            _resident((D_MODEL, 2 * CONV_CH)),
            _resident((CONV_K, CONV_CH)),
            _resident((1, CONV_CH)),
            _resident((1, CONV_CH)),
            _resident((1, CONV_CH)),
        ],
        out_specs=[
            pl.BlockSpec((None, tm, ATTN_WIDTH), lambda bi, i: (bi, i, 0)),
            pl.BlockSpec((None, tm, 2 * KV_WIDTH), lambda bi, i: (bi, i, 0)),
            pl.BlockSpec((None, tm, 2 * KV_WIDTH), lambda bi, i: (bi, i, 0)),
            pl.BlockSpec((None, tm, CONV_CH), lambda bi, i: (bi, i, 0)),
        ],
        out_shape=[
            jax.ShapeDtypeStruct((b, s, ATTN_WIDTH), BF16),
            jax.ShapeDtypeStruct((b, s, 2 * KV_WIDTH), BF16),
            jax.ShapeDtypeStruct((b, s, 2 * KV_WIDTH), BF16),
            jax.ShapeDtypeStruct((b, s, CONV_CH), BF16),
        ],
        scratch_shapes=[
            pltpu.VMEM((tm + 2 * G_HALO, D_MODEL), BF16),
            pltpu.VMEM((N_SLAB, tm + 2 * G_HALO, LANES), F32),
            pltpu.VMEM((tm, CONV_CH), F32),
        ],
        compiler_params=pltpu.CompilerParams(
            dimension_semantics=("arbitrary", "arbitrary"), vmem_limit_bytes=VMEM_LIMIT_BYTES),
        name="inproj",
    )(x, x, x, g, wqkv, wc, cw, cb, lng, lnb)


def _mixer_kernel(sink_ref, x_ref, q_ref, k_ref, v_ref, c_ref, bias_ref, woa_ref, woc_ref, out_ref,
                  kz_ref, vz_ref, y_ref, sa_ref, sb_ref, *, seq):
    i = pl.program_id(1)
    r0 = i * TQ
    kv_rows = TQ + 2 * BLOCK
    n_blk = TQ // BLOCK

    ws = pl.multiple_of(jnp.clip(r0 - BLOCK, 0, seq - kv_rows), BLOCK)
    lo = lax.broadcasted_iota(jnp.int32, (1, LANES), 1) < HEAD_DIM
    for src_ref, dst_ref, fill in ((k_ref, kz_ref, jnp.zeros((), BF16)), (v_ref, vz_ref, jnp.ones((), BF16))):
        w = src_ref[pl.ds(ws, kv_rows), :]
        straight, swapped = w[:, :KV_WIDTH], w[:, KV_WIDTH:]
        dst_ref[0] = jnp.where(lo, straight, fill)
        dst_ref[1] = jnp.where(lo, fill, swapped)
        dst_ref[2] = jnp.where(lo, swapped, fill)
        dst_ref[3] = jnp.where(lo, fill, straight)

    dyn0 = pl.multiple_of(jnp.minimum(i, 0) * BLOCK, BLOCK)

    def block_pos(jb):
        r = r0 + jb * BLOCK
        koff = pl.multiple_of(jnp.clip(r - BLOCK, 0, seq - BAND) - ws, BLOCK)
        var = jnp.where(r == 0, 1, jnp.where(r == seq - BLOCK, 2, 0))
        return koff, var

    def score_stage(jb, s_ref):
        koff, _ = block_pos(jb)
        row = jb * BLOCK
        for n in range(N_HEADS):
            m, e = divmod(n, 2)
            h = m // (GROUP // 2)
            qp = q_ref[row:row + BLOCK, m * LANES:(m + 1) * LANES]
            kz = kz_ref[2 * h + e, pl.ds(koff, BAND), :]
            s_ref[n] = lax.dot_general(qp, kz, (((1,), (1,)), ((), ())), preferred_element_type=F32)

    def value_stage(jb, s_ref):
        koff, var = block_pos(jb)
        row = jb * BLOCK
        for m in range(N_HEADS // 2):
            h = m // (GROUP // 2)
            pv = []
            sk = []
            for e in range(2):
                n = 2 * m + e
                s = s_ref[n, pl.ds(dyn0, BLOCK), :] + bias_ref[var, n]
                sink = sink_ref[n] * LOG2E
                mx = jnp.maximum(jnp.max(s, axis=-1, keepdims=True), sink)
                p = jnp.exp2(s - mx)
                sk.append(jnp.broadcast_to(jnp.exp2(sink - mx), (BLOCK, LANES)))
                vz = vz_ref[2 * h + e, pl.ds(koff, BAND), :]
                pv.append(jnp.dot(p.astype(BF16), vz, preferred_element_type=F32))
            outs = jnp.where(lo, pv[0], pv[1])
            sums = pltpu.roll(jnp.where(lo, pv[1], pv[0]), HEAD_DIM, axis=1)
            denom = sums + jnp.where(lo, sk[0], sk[1])
            y_ref[row:row + BLOCK, m * LANES:(m + 1) * LANES] = (outs / denom).astype(BF16)

    def out_proj(jb):
        rows = slice(jb * BLOCK, (jb + 1) * BLOCK)
        out_ref[rows, :] = (x_ref[rows, :]
                            + jnp.dot(y_ref[rows, :], woa_ref[...], preferred_element_type=F32)
                            + jnp.dot(c_ref[rows, :], woc_ref[...], preferred_element_type=F32))

    stage = (sa_ref, sb_ref)
    score_stage(0, stage[0])
    for jb in range(n_blk):
        if jb + 1 < n_blk:
            score_stage(jb + 1, stage[(jb + 1) % 2])
        if jb > 0:
            out_proj(jb - 1)
        value_stage(jb, stage[jb % 2])
    out_proj(n_blk - 1)


def _mixer(x, q, k, v, c_act, tables, sink, woa, woc):
    b, s, _ = x.shape
    kernel = functools.partial(_mixer_kernel, seq=s)
    return pl.pallas_call(
        kernel,
        grid=(b, s // TQ),
        in_specs=[
            pl.BlockSpec(memory_space=pltpu.SMEM),
            pl.BlockSpec((None, TQ, D_MODEL), lambda bi, i: (bi, i, 0)),
            pl.BlockSpec((None, TQ, ATTN_WIDTH), lambda bi, i: (bi, i, 0)),
            pl.BlockSpec((None, s, 2 * KV_WIDTH), lambda bi, i: (bi, 0, 0), pipeline_mode=pl.Buffered(1)),
            pl.BlockSpec((None, s, 2 * KV_WIDTH), lambda bi, i: (bi, 0, 0), pipeline_mode=pl.Buffered(1)),
            pl.BlockSpec((None, TQ, CONV_CH), lambda bi, i: (bi, i, 0)),
            _resident((N_VARIANT, N_HEADS, BLOCK, BAND)),
            _resident((ATTN_WIDTH, D_MODEL)),
            _resident((CONV_CH, D_MODEL)),
        ],
        out_specs=pl.BlockSpec((None, TQ, D_MODEL), lambda bi, i: (bi, i, 0)),
        out_shape=jax.ShapeDtypeStruct((b, s, D_MODEL), F32),
        scratch_shapes=[
            pltpu.VMEM((2 * N_KV, TQ + 2 * BLOCK, LANES), BF16),
            pltpu.VMEM((2 * N_KV, TQ + 2 * BLOCK, LANES), BF16),
            pltpu.VMEM((TQ, ATTN_WIDTH), BF16),
            pltpu.VMEM((N_HEADS, BLOCK, BAND), F32),
            pltpu.VMEM((N_HEADS, BLOCK, BAND), F32),
        ],
        compiler_params=pltpu.CompilerParams(
            dimension_semantics=("arbitrary", "arbitrary"), vmem_limit_bytes=VMEM_LIMIT_BYTES),
        name="mixer",
    )(sink, x, q, k, v, c_act, tables, woa, woc)


def _ffn_kernel(x_ref, xp_ref, xn_ref, g_ref, wup_ref, dw_ref, db_ref, wdn_ref, gf_ref, out_ref,
                hb_ref, uga_ref, uva_ref, ugb_ref, uvb_ref, *, final_norm):
    i = pl.program_id(1)
    first = i == 0
    last = i == pl.num_programs(1) - 1
    tm = TM_FF
    g = g_ref[...]
    hb_ref[:U_HALO, :] = jnp.where(first, 0.0, _rmsnorm(xp_ref[...], g)).astype(BF16)
    hb_ref[U_HALO:U_HALO + tm, :] = _rmsnorm(x_ref[...], g).astype(BF16)
    hb_ref[U_HALO + tm:, :] = jnp.where(last, 0.0, _rmsnorm(xn_ref[...], g)).astype(BF16)
    out_ref[...] = x_ref[...]

    def up_proj(j, ug_ref, uv_ref):
        hb = hb_ref[...]
        ug = jnp.dot(hb, wup_ref[0, j], preferred_element_type=F32)
        uv = jnp.dot(hb, wup_ref[1, j], preferred_element_type=F32)
        for c in range(CJ // LANES):
            ug_ref[c] = ug[:, c * LANES:(c + 1) * LANES]
            uv_ref[c] = uv[:, c * LANES:(c + 1) * LANES]

    def conv3(u_ref, c, w, bias):
        lanes = slice(c * LANES, (c + 1) * LANES)
        out = bias[:, lanes] + u_ref[c, U_HALO - 1:U_HALO - 1 + tm, :] * w[0:1, lanes]
        out = out + u_ref[c, U_HALO:U_HALO + tm, :] * w[1:2, lanes]
        return out + u_ref[c, U_HALO + 1:U_HALO + 1 + tm, :] * w[2:3, lanes]

    def down_proj(j, ug_ref, uv_ref):
        acts = []
        for c in range(CJ // LANES):
            gate = conv3(ug_ref, c, dw_ref[0, j], db_ref[0, j])
            val = conv3(uv_ref, c, dw_ref[1, j], db_ref[1, j])
            acts.append(((gate * _sigmoid(gate)) * val).astype(BF16))
        act = jnp.concatenate(acts, axis=1)
        out_ref[...] += jnp.dot(act, wdn_ref[j], preferred_element_type=F32)

    up_proj(0, uga_ref, uva_ref)

    def chunk_pair(j):
        up_proj(j + 1, ugb_ref, uvb_ref)
        down_proj(j, uga_ref, uva_ref)
        up_proj(j + 2, uga_ref, uva_ref)
        down_proj(j + 1, ugb_ref, uvb_ref)

    def chunk_quad(p, carry):
        chunk_pair(4 * p)
        chunk_pair(4 * p + 2)
        return carry

    n_quad = (NJ - 1) // 4
    lax.fori_loop(0, n_quad, chunk_quad, 0)
    for j in range(4 * n_quad, NJ - 1, 2):
        chunk_pair(j)
    down_proj(NJ - 1, uga_ref, uva_ref)
    if final_norm:
        out_ref[...] = _rmsnorm(out_ref[...], gf_ref[...])


def _ffn(x, g, wup, dw, db, wdn, gf, final_norm):
    b, s, _ = x.shape
    tm = TM_FF
    kernel = functools.partial(_ffn_kernel, final_norm=final_norm)
    return pl.pallas_call(
        kernel,
        grid=(b, s // tm),
        in_specs=_halo_specs(tm, U_HALO, s) + [
            _resident((1, D_MODEL)),
            _resident((2, NJ, D_MODEL, CJ)),
            _resident((2, NJ, FFN_CONV_K, CJ)),
            _resident((2, NJ, 1, CJ)),
            _resident((NJ, CJ, D_MODEL)),
            _resident((1, D_MODEL)),
        ],
        out_specs=pl.BlockSpec((None, tm, D_MODEL), lambda bi, i: (bi, i, 0)),
        out_shape=jax.ShapeDtypeStruct((b, s, D_MODEL), F32),
        scratch_shapes=[pltpu.VMEM((tm + 2 * U_HALO, D_MODEL), BF16)]
        + [pltpu.VMEM((CJ // LANES, tm + 2 * U_HALO, LANES), F32)] * 4,
        compiler_params=pltpu.CompilerParams(
            dimension_semantics=("arbitrary", "arbitrary"), vmem_limit_bytes=VMEM_LIMIT_BYTES),
        name="ffn",
    )(x, x, x, g, wup, dw, db, wdn, gf)


def _split_ff(a):
    lead = a.shape[:-1]
    a = a.reshape(lead + (2, NJ, CJ))
    return jnp.moveaxis(a, (-3, -2), (0, 1))


def _prepare_layer(l, norm_attn_g, w_in, attn_sink, conv_dw_w, conv_dw_b, conv_ln_g, conv_ln_b, w_out,
                   norm_ffn_g, w_up, ffn_dw_w, ffn_dw_b, w_down):
    return dict(
        g_attn=norm_attn_g[l].astype(F32)[None, :],
        wqkv=w_in[l][:, :QKV_WIDTH].astype(BF16),
        wc=w_in[l][:, QKV_WIDTH:].astype(BF16),
        sink=attn_sink[l].astype(F32),
        cw=conv_dw_w[l].astype(F32),
        cb=conv_dw_b[l].astype(F32)[None, :],
        lng=conv_ln_g[l].astype(F32)[None, :],
        lnb=conv_ln_b[l].astype(F32)[None, :],
        woa=w_out[l][:ATTN_WIDTH].astype(BF16),
        woc=w_out[l][ATTN_WIDTH:].astype(BF16),
        g_ffn=norm_ffn_g[l].astype(F32)[None, :],
        wup=_split_ff(w_up[l].astype(BF16)),
        dw=_split_ff(ffn_dw_w[l].astype(F32)),
        db=_split_ff(ffn_dw_b[l].astype(F32)[None, :]),
        wdn=w_down[l].astype(BF16).reshape(NJ, CJ, D_MODEL),
    )


def _trunk(x, tables, layers, gf):
    depth = len(layers)
    for l, p in enumerate(layers):
        q, k, v, c_act = _inproj(x, p["g_attn"], p["wqkv"], p["wc"], p["cw"], p["cb"], p["lng"], p["lnb"])
        x = _mixer(x, q, k, v, c_act, tables, p["sink"], p["woa"], p["woc"])
        x = _ffn(x, p["g_ffn"], p["wup"], p["dw"], p["db"], p["wdn"], gf, final_norm=(l == depth - 1))
    return x


def kernel(x_prompt, x_sample, rel_bias, norm_attn_g, w_in, attn_sink, conv_dw_w, conv_dw_b, conv_ln_g,
           conv_ln_b, w_out, norm_ffn_g, w_up, ffn_dw_w, ffn_dw_b, w_down, norm_final_g):
    depth = w_in.shape[0]
    tables = _band_tables(rel_bias)
    layers = [
        _prepare_layer(l, norm_attn_g, w_in, attn_sink, conv_dw_w, conv_dw_b, conv_ln_g, conv_ln_b, w_out,
                       norm_ffn_g, w_up, ffn_dw_w, ffn_dw_b, w_down)
        for l in range(depth)
    ]
    gf = norm_final_g.astype(F32)[None, :]
    return (_trunk(x_prompt, tables, layers, gf), _trunk(x_sample, tables, layers, gf))
```

```python
import functools
import math

import jax
import jax.numpy as jnp
from jax import lax
from jax.experimental import pallas as pl
from jax.experimental.pallas import tpu as pltpu

F32 = jnp.float32
BF16 = jnp.bfloat16

D_MODEL = 1024
HEAD_DIM = 64
N_HEADS = 8
N_KV = 2
GROUP = N_HEADS // N_KV
ATTN_WIDTH = N_HEADS * HEAD_DIM
KV_WIDTH = N_KV * HEAD_DIM
CONV_CH = D_MODEL - ATTN_WIDTH
CONV_K = 31
CONV_PAD = (CONV_K - 1) // 2
WINDOW = 128
BLOCK = 128
BAND = 3 * BLOCK
N_BUCKETS = 32
MAX_DISTANCE = 128
D_FF = 2816
FFN_CONV_K = 3
QKV_WIDTH = ATTN_WIDTH + 2 * KV_WIDTH
EPS = 1e-6
NEG = -1e30
SCALE = HEAD_DIM ** -0.5
LOG2E = math.log2(math.e)

LANES = 128
SUBLANES = 8
VMEM_LIMIT_BYTES = 56 * 1024 * 1024

TM_IN = 2048
TQ = 1024
TM_FF = 1024
CJ = 256
NJ = D_FF // CJ
N_SLAB = CONV_CH // LANES
SB = 128
CONV_ROWS = 32
G_HALO = 16
U_HALO = SUBLANES
N_VARIANT = 3


def _sigmoid(x):
    return 1.0 / (1.0 + jnp.exp2(x * -LOG2E))


def _rmsnorm(x, g):
    ms = jnp.mean(x * x, axis=-1, keepdims=True)
    return (x * lax.rsqrt(ms + EPS)) * g


def _resident(shape):
    return pl.BlockSpec(shape, lambda bi, i: (0,) * len(shape), pipeline_mode=pl.Buffered(1))


def _halo_specs(tm, halo, seq):
    per_tile = tm // halo
    n_halo = seq // halo
    return [
        pl.BlockSpec((None, tm, D_MODEL), lambda bi, i: (bi, i, 0)),
        pl.BlockSpec((None, halo, D_MODEL), lambda bi, i: (bi, jnp.maximum(i * per_tile - 1, 0), 0)),
        pl.BlockSpec((None, halo, D_MODEL), lambda bi, i: (bi, jnp.minimum((i + 1) * per_tile, n_halo - 1), 0)),
    ]


def _bias_table_kernel(rb_ref, bucket_ref, valid_ref, out_ref):
    bucket = bucket_ref[...]
    valid = valid_ref[...] != 0
    for n in range(N_HEADS):
        val = jnp.zeros(bucket.shape, F32)
        for b in range(N_BUCKETS):
            val = jnp.where(bucket == b, rb_ref[b, n], val)
        out_ref[n] = jnp.where(valid, val * LOG2E, NEG)


def _t5_buckets(rel):
    nb = N_BUCKETS // 2
    ret = (rel > 0).astype(jnp.int32) * nb
    n = jnp.abs(rel)
    max_exact = nb // 2
    nf = jnp.maximum(n, 1).astype(F32)
    large = max_exact + (jnp.log(nf / max_exact) / math.log(MAX_DISTANCE / max_exact)
                         * (nb - max_exact)).astype(jnp.int32)
    large = jnp.minimum(large, nb - 1)
    return ret + jnp.where(n < max_exact, n, large)


def _band_tables(rel_bias):
    qi = jnp.arange(BLOCK, dtype=jnp.int32)[:, None]
    kj = jnp.arange(BAND, dtype=jnp.int32)[None, :]
    offsets = jnp.array([-BLOCK, 0, -2 * BLOCK], jnp.int32)[:, None, None]
    rel = (kj - qi)[None] + offsets
    bucket = _t5_buckets(rel)
    valid = (jnp.abs(rel) <= WINDOW).astype(jnp.int32)
    return pl.pallas_call(
        _bias_table_kernel,
        grid=(N_VARIANT,),
        in_specs=[
            pl.BlockSpec(memory_space=pltpu.SMEM),
            pl.BlockSpec((None, BLOCK, BAND), lambda v: (v, 0, 0)),
            pl.BlockSpec((None, BLOCK, BAND), lambda v: (v, 0, 0)),
        ],
        out_specs=pl.BlockSpec((None, N_HEADS, BLOCK, BAND), lambda v: (v, 0, 0, 0)),
        out_shape=jax.ShapeDtypeStruct((N_VARIANT, N_HEADS, BLOCK, BAND), F32),
        name="band_bias_tables",
    )(rel_bias.astype(F32), bucket, valid)


def _inproj_kernel(x_ref, xp_ref, xn_ref, g_ref, wqkv_ref, wc_ref, cw_ref, cb_ref, lng_ref, lnb_ref,
                   q_ref, k_ref, v_ref, c_ref, hb_ref, gl_ref, cv_ref):
    i = pl.program_id(1)
    first = i == 0
    last = i == pl.num_programs(1) - 1
    tm = TM_IN
    g = g_ref[...]

    def norm_rows(lo, hi):
        if lo < G_HALO:
            hb_ref[lo:G_HALO, :] = jnp.where(first, 0.0, _rmsnorm(xp_ref[lo:, :], g)).astype(BF16)
        m_lo, m_hi = max(lo, G_HALO), min(hi, G_HALO + tm)
        hb_ref[m_lo:m_hi, :] = _rmsnorm(x_ref[m_lo - G_HALO:m_hi - G_HALO, :], g).astype(BF16)
        if hi > G_HALO + tm:
            hb_ref[G_HALO + tm:hi, :] = jnp.where(last, 0.0, _rmsnorm(xn_ref[:hi - G_HALO - tm, :], g)).astype(BF16)

    def glu_rows(lo, hi):
        zc = jnp.dot(hb_ref[lo:hi, :], wc_ref[...], preferred_element_type=F32)
        glu = zc[:, :CONV_CH] * _sigmoid(zc[:, CONV_CH:])
        for c in range(N_SLAB):
            gl_ref[c, lo:hi, :] = glu[:, c * LANES:(c + 1) * LANES]

    def qkv_rows(r):
        z = jnp.dot(hb_ref[G_HALO + r:G_HALO + r + SB, :], wqkv_ref[...], preferred_element_type=F32)
        q_ref[r:r + SB, :] = (z[:, :ATTN_WIDTH] * (SCALE * LOG2E)).astype(BF16)
        kf = z[:, ATTN_WIDTH:ATTN_WIDTH + KV_WIDTH]
        vf = z[:, ATTN_WIDTH + KV_WIDTH:]
        k_ref[r:r + SB, :KV_WIDTH] = kf.astype(BF16)
        k_ref[r:r + SB, KV_WIDTH:] = pltpu.roll(kf, HEAD_DIM, axis=1).astype(BF16)
        v_ref[r:r + SB, :KV_WIDTH] = vf.astype(BF16)
        v_ref[r:r + SB, KV_WIDTH:] = pltpu.roll(vf, HEAD_DIM, axis=1).astype(BF16)

    def conv_rows(r):
        for c in range(N_SLAB):
            lanes = slice(c * LANES, (c + 1) * LANES)
            for rr in range(r, r + SB, CONV_ROWS):
                acc = jnp.broadcast_to(cb_ref[:, lanes], (CONV_ROWS, LANES))
                for t in range(CONV_K):
                    lo = rr + (G_HALO - CONV_PAD) + t
                    acc = acc + gl_ref[c, lo:lo + CONV_ROWS, :] * cw_ref[t:t + 1, lanes]
                cv_ref[rr:rr + CONV_ROWS, lanes] = acc
        cv = cv_ref[r:r + SB, :]
        mu = jnp.mean(cv, axis=-1, keepdims=True)
        xc = cv - mu
        ln = (xc * lax.rsqrt(jnp.mean(xc * xc, axis=-1, keepdims=True) + EPS)) * lng_ref[...] + lnb_ref[...]
        c_ref[r:r + SB, :] = (ln * _sigmoid(ln)).astype(BF16)

    n_sb = tm // SB
    bounds = [0] + [(k + 1) * SB + 2 * G_HALO for k in range(n_sb)]
    norm_rows(bounds[0], bounds[1])
    glu_rows(bounds[0], bounds[1])
    for sb in range(n_sb):
        if sb + 1 < n_sb:
            norm_rows(bounds[sb + 1], bounds[sb + 2])
            glu_rows(bounds[sb + 1], bounds[sb + 2])
        qkv_rows(sb * SB)
        conv_rows(sb * SB)


def _inproj(x, g, wqkv, wc, cw, cb, lng, lnb):
    b, s, _ = x.shape
    tm = TM_IN
    return pl.pallas_call(
        _inproj_kernel,
        grid=(b, s // tm),
        in_specs=_halo_specs(tm, G_HALO, s) + [
            _resident((1, D_MODEL)),
            _resident((D_MODEL, QKV_WIDTH)),
            _resident((D_MODEL, 2 * CONV_CH)),
            _resident((CONV_K, CONV_CH)),
            _resident((1, CONV_CH)),
            _resident((1, CONV_CH)),
            _resident((1, CONV_CH)),
        ],
        out_specs=[
            pl.BlockSpec((None, tm, ATTN_WIDTH), lambda bi, i: (bi, i, 0)),
            pl.BlockSpec((None, tm, 2 * KV_WIDTH), lambda bi, i: (bi, i, 0)),
            pl.BlockSpec((None, tm, 2 * KV_WIDTH), lambda bi, i: (bi, i, 0)),
            pl.BlockSpec((None, tm, CONV_CH), lambda bi, i: (bi, i, 0)),
        ],
        out_shape=[
            jax.ShapeDtypeStruct((b, s, ATTN_WIDTH), BF16),
            jax.ShapeDtypeStruct((b, s, 2 * KV_WIDTH), BF16),
            jax.ShapeDtypeStruct((b, s, 2 * KV_WIDTH), BF16),
            jax.ShapeDtypeStruct((b, s, CONV_CH), BF16),
        ],
        scratch_shapes=[
            pltpu.VMEM((tm + 2 * G_HALO, D_MODEL), BF16),
            pltpu.VMEM((N_SLAB, tm + 2 * G_HALO, LANES), F32),
            pltpu.VMEM((tm, CONV_CH), F32),
        ],
        compiler_params=pltpu.CompilerParams(
            dimension_semantics=("arbitrary", "arbitrary"), vmem_limit_bytes=VMEM_LIMIT_BYTES),
        name="inproj",
    )(x, x, x, g, wqkv, wc, cw, cb, lng, lnb)


def _mixer_kernel(sink_ref, x_ref, q_ref, k_ref, v_ref, c_ref, bias_ref, woa_ref, woc_ref, out_ref,
                  kz_ref, vz_ref, y_ref, sa_ref, sb_ref, *, seq):
    i = pl.program_id(1)
    r0 = i * TQ
    kv_rows = TQ + 2 * BLOCK
    n_blk = TQ // BLOCK

    ws = pl.multiple_of(jnp.clip(r0 - BLOCK, 0, seq - kv_rows), BLOCK)
    lo = lax.broadcasted_iota(jnp.int32, (1, LANES), 1) < HEAD_DIM
    for src_ref, dst_ref, fill in ((k_ref, kz_ref, jnp.zeros((), BF16)), (v_ref, vz_ref, jnp.ones((), BF16))):
        w = src_ref[pl.ds(ws, kv_rows), :]
        straight, swapped = w[:, :KV_WIDTH], w[:, KV_WIDTH:]
        dst_ref[0] = jnp.where(lo, straight, fill)
        dst_ref[1] = jnp.where(lo, fill, swapped)
        dst_ref[2] = jnp.where(lo, swapped, fill)
        dst_ref[3] = jnp.where(lo, fill, straight)

    dyn0 = pl.multiple_of(jnp.minimum(i, 0) * BLOCK, BLOCK)

    def block_pos(jb):
        r = r0 + jb * BLOCK
        koff = pl.multiple_of(jnp.clip(r - BLOCK, 0, seq - BAND) - ws, BLOCK)
        var = jnp.where(r == 0, 1, jnp.where(r == seq - BLOCK, 2, 0))
        return koff, var

    def score_stage(jb, s_ref):
        koff, _ = block_pos(jb)
        row = jb * BLOCK
        for n in range(N_HEADS):
            m, e = divmod(n, 2)
            h = m // (GROUP // 2)
            qp = q_ref[row:row + BLOCK, m * LANES:(m + 1) * LANES]
            kz = kz_ref[2 * h + e, pl.ds(koff, BAND), :]
            s_ref[n] = lax.dot_general(qp, kz, (((1,), (1,)), ((), ())), preferred_element_type=F32)

    def value_stage(jb, s_ref):
        koff, var = block_pos(jb)
        row = jb * BLOCK
        for m in range(N_HEADS // 2):
            h = m // (GROUP // 2)
            pv = []
            sk = []
            for e in range(2):
                n = 2 * m + e
                s = s_ref[n, pl.ds(dyn0, BLOCK), :] + bias_ref[var, n]
                sink = sink_ref[n] * LOG2E
                mx = jnp.maximum(jnp.max(s, axis=-1, keepdims=True), sink)
                p = jnp.exp2(s - mx)
                sk.append(jnp.broadcast_to(jnp.exp2(sink - mx), (BLOCK, LANES)))
                vz = vz_ref[2 * h + e, pl.ds(koff, BAND), :]
                pv.append(jnp.dot(p.astype(BF16), vz, preferred_element_type=F32))
            outs = jnp.where(lo, pv[0], pv[1])
            sums = pltpu.roll(jnp.where(lo, pv[1], pv[0]), HEAD_DIM, axis=1)
            denom = sums + jnp.where(lo, sk[0], sk[1])
            y_ref[row:row + BLOCK, m * LANES:(m + 1) * LANES] = (outs / denom).astype(BF16)

    def out_proj(jb):
        rows = slice(jb * BLOCK, (jb + 1) * BLOCK)
        out_ref[rows, :] = (x_ref[rows, :]
                            + jnp.dot(y_ref[rows, :], woa_ref[...], preferred_element_type=F32)
                            + jnp.dot(c_ref[rows, :], woc_ref[...], preferred_element_type=F32))

    stage = (sa_ref, sb_ref)
    score_stage(0, stage[0])
    for jb in range(n_blk):
        if jb + 1 < n_blk:
            score_stage(jb + 1, stage[(jb + 1) % 2])
        if jb > 0:
            out_proj(jb - 1)
        value_stage(jb, stage[jb % 2])
    out_proj(n_blk - 1)


def _mixer(x, q, k, v, c_act, tables, sink, woa, woc):
    b, s, _ = x.shape
    kernel = functools.partial(_mixer_kernel, seq=s)
    return pl.pallas_call(
        kernel,
        grid=(b, s // TQ),
        in_specs=[
            pl.BlockSpec(memory_space=pltpu.SMEM),
            pl.BlockSpec((None, TQ, D_MODEL), lambda bi, i: (bi, i, 0)),
            pl.BlockSpec((None, TQ, ATTN_WIDTH), lambda bi, i: (bi, i, 0)),
            pl.BlockSpec((None, s, 2 * KV_WIDTH), lambda bi, i: (bi, 0, 0)),
            pl.BlockSpec((None, s, 2 * KV_WIDTH), lambda bi, i: (bi, 0, 0)),
            pl.BlockSpec((None, TQ, CONV_CH), lambda bi, i: (bi, i, 0)),
            _resident((N_VARIANT, N_HEADS, BLOCK, BAND)),
            _resident((ATTN_WIDTH, D_MODEL)),
            _resident((CONV_CH, D_MODEL)),
        ],
        out_specs=pl.BlockSpec((None, TQ, D_MODEL), lambda bi, i: (bi, i, 0)),
        out_shape=jax.ShapeDtypeStruct((b, s, D_MODEL), F32),
        scratch_shapes=[
            pltpu.VMEM((2 * N_KV, TQ + 2 * BLOCK, LANES), BF16),
            pltpu.VMEM((2 * N_KV, TQ + 2 * BLOCK, LANES), BF16),
            pltpu.VMEM((TQ, ATTN_WIDTH), BF16),
            pltpu.VMEM((N_HEADS, BLOCK, BAND), F32),
            pltpu.VMEM((N_HEADS, BLOCK, BAND), F32),
        ],
        compiler_params=pltpu.CompilerParams(
            dimension_semantics=("arbitrary", "arbitrary"), vmem_limit_bytes=VMEM_LIMIT_BYTES),
        name="mixer",
    )(sink, x, q, k, v, c_act, tables, woa, woc)


def _ffn_kernel(x_ref, xp_ref, xn_ref, g_ref, wup_ref, dw_ref, db_ref, wdn_ref, gf_ref, out_ref,
                hb_ref, uga_ref, uva_ref, ugb_ref, uvb_ref, *, final_norm):
    i = pl.program_id(1)
    first = i == 0
    last = i == pl.num_programs(1) - 1
    tm = TM_FF
    g = g_ref[...]
    hb_ref[:U_HALO, :] = jnp.where(first, 0.0, _rmsnorm(xp_ref[...], g)).astype(BF16)
    hb_ref[U_HALO:U_HALO + tm, :] = _rmsnorm(x_ref[...], g).astype(BF16)
    hb_ref[U_HALO + tm:, :] = jnp.where(last, 0.0, _rmsnorm(xn_ref[...], g)).astype(BF16)
    out_ref[...] = x_ref[...]

    def up_proj(j, ug_ref, uv_ref):
        hb = hb_ref[...]
        ug = jnp.dot(hb, wup_ref[0, j], preferred_element_type=F32)
        uv = jnp.dot(hb, wup_ref[1, j], preferred_element_type=F32)
        for c in range(CJ // LANES):
            ug_ref[c] = ug[:, c * LANES:(c + 1) * LANES]
            uv_ref[c] = uv[:, c * LANES:(c + 1) * LANES]

    def conv3(u_ref, c, w, bias):
        lanes = slice(c * LANES, (c + 1) * LANES)
        out = bias[:, lanes] + u_ref[c, U_HALO - 1:U_HALO - 1 + tm, :] * w[0:1, lanes]
        out = out + u_ref[c, U_HALO:U_HALO + tm, :] * w[1:2, lanes]
        return out + u_ref[c, U_HALO + 1:U_HALO + 1 + tm, :] * w[2:3, lanes]

    def down_proj(j, ug_ref, uv_ref):
        acts = []
        for c in range(CJ // LANES):
            gate = conv3(ug_ref, c, dw_ref[0, j], db_ref[0, j])
            val = conv3(uv_ref, c, dw_ref[1, j], db_ref[1, j])
            acts.append(((gate * _sigmoid(gate)) * val).astype(BF16))
        act = jnp.concatenate(acts, axis=1)
        out_ref[...] += jnp.dot(act, wdn_ref[j], preferred_element_type=F32)

    up_proj(0, uga_ref, uva_ref)

    def chunk_pair(j):
        up_proj(j + 1, ugb_ref, uvb_ref)
        down_proj(j, uga_ref, uva_ref)
        up_proj(j + 2, uga_ref, uva_ref)
        down_proj(j + 1, ugb_ref, uvb_ref)

    def chunk_quad(p, carry):
        chunk_pair(4 * p)
        chunk_pair(4 * p + 2)
        return carry

    n_quad = (NJ - 1) // 4
    lax.fori_loop(0, n_quad, chunk_quad, 0)
    for j in range(4 * n_quad, NJ - 1, 2):
        chunk_pair(j)
    down_proj(NJ - 1, uga_ref, uva_ref)
    if final_norm:
        out_ref[...] = _rmsnorm(out_ref[...], gf_ref[...])


def _ffn(x, g, wup, dw, db, wdn, gf, final_norm):
    b, s, _ = x.shape
    tm = TM_FF
    kernel = functools.partial(_ffn_kernel, final_norm=final_norm)
    return pl.pallas_call(
        kernel,
        grid=(b, s // tm),
        in_specs=_halo_specs(tm, U_HALO, s) + [
            _resident((1, D_MODEL)),
            _resident((2, NJ, D_MODEL, CJ)),
            _resident((2, NJ, FFN_CONV_K, CJ)),
            _resident((2, NJ, 1, CJ)),
            _resident((NJ, CJ, D_MODEL)),
            _resident((1, D_MODEL)),
        ],
        out_specs=pl.BlockSpec((None, tm, D_MODEL), lambda bi, i: (bi, i, 0)),
        out_shape=jax.ShapeDtypeStruct((b, s, D_MODEL), F32),
        scratch_shapes=[pltpu.VMEM((tm + 2 * U_HALO, D_MODEL), BF16)]
        + [pltpu.VMEM((CJ // LANES, tm + 2 * U_HALO, LANES), F32)] * 4,
        compiler_params=pltpu.CompilerParams(
            dimension_semantics=("arbitrary", "arbitrary"), vmem_limit_bytes=VMEM_LIMIT_BYTES),
        name="ffn",
    )(x, x, x, g, wup, dw, db, wdn, gf)


def _split_ff(a):
    lead = a.shape[:-1]
    a = a.reshape(lead + (2, NJ, CJ))
    return jnp.moveaxis(a, (-3, -2), (0, 1))


def _prepare_layer(l, norm_attn_g, w_in, attn_sink, conv_dw_w, conv_dw_b, conv_ln_g, conv_ln_b, w_out,
                   norm_ffn_g, w_up, ffn_dw_w, ffn_dw_b, w_down):
    return dict(
        g_attn=norm_attn_g[l].astype(F32)[None, :],
        wqkv=w_in[l][:, :QKV_WIDTH].astype(BF16),
        wc=w_in[l][:, QKV_WIDTH:].astype(BF16),
        sink=attn_sink[l].astype(F32),
        cw=conv_dw_w[l].astype(F32),
        cb=conv_dw_b[l].astype(F32)[None, :],
        lng=conv_ln_g[l].astype(F32)[None, :],
        lnb=conv_ln_b[l].astype(F32)[None, :],
        woa=w_out[l][:ATTN_WIDTH].astype(BF16),
        woc=w_out[l][ATTN_WIDTH:].astype(BF16),
        g_ffn=norm_ffn_g[l].astype(F32)[None, :],
        wup=_split_ff(w_up[l].astype(BF16)),
        dw=_split_ff(ffn_dw_w[l].astype(F32)),
        db=_split_ff(ffn_dw_b[l].astype(F32)[None, :]),
        wdn=w_down[l].astype(BF16).reshape(NJ, CJ, D_MODEL),
    )


def _trunk(x, tables, layers, gf):
    depth = len(layers)
    for l, p in enumerate(layers):
        q, k, v, c_act = _inproj(x, p["g_attn"], p["wqkv"], p["wc"], p["cw"], p["cb"], p["lng"], p["lnb"])
        x = _mixer(x, q, k, v, c_act, tables, p["sink"], p["woa"], p["woc"])
        x = _ffn(x, p["g_ffn"], p["wup"], p["dw"], p["db"], p["wdn"], gf, final_norm=(l == depth - 1))
    return x


def kernel(x_prompt, x_sample, rel_bias, norm_attn_g, w_in, attn_sink, conv_dw_w, conv_dw_b, conv_ln_g,
           conv_ln_b, w_out, norm_ffn_g, w_up, ffn_dw_w, ffn_dw_b, w_down, norm_final_g):
    depth = w_in.shape[0]
    tables = _band_tables(rel_bias)
    layers = [
        _prepare_layer(l, norm_attn_g, w_in, attn_sink, conv_dw_w, conv_dw_b, conv_ln_g, conv_ln_b, w_out,
                       norm_ffn_g, w_up, ffn_dw_w, ffn_dw_b, w_down)
        for l in range(depth)
    ]
    gf = norm_final_g.astype(F32)[None, :]
    return (_trunk(x_prompt, tables, layers, gf), _trunk(x_sample, tables, layers, gf))
```

```python
import functools
import math

import jax
import jax.numpy as jnp
from jax import lax
from jax.experimental import pallas as pl
from jax.experimental.pallas import tpu as pltpu

F32 = jnp.float32
BF16 = jnp.bfloat16

D_MODEL = 1024
HEAD_DIM = 64
N_HEADS = 8
N_KV = 2
GROUP = N_HEADS // N_KV
ATTN_WIDTH = N_HEADS * HEAD_DIM
KV_WIDTH = N_KV * HEAD_DIM
CONV_CH = D_MODEL - ATTN_WIDTH
CONV_K = 31
CONV_PAD = (CONV_K - 1) // 2
WINDOW = 128
BLOCK = 128
BAND = 3 * BLOCK
N_BUCKETS = 32
MAX_DISTANCE = 128
D_FF = 2816
FFN_CONV_K = 3
QKV_WIDTH = ATTN_WIDTH + 2 * KV_WIDTH
EPS = 1e-6
NEG = -1e30
SCALE = HEAD_DIM ** -0.5
LOG2E = math.log2(math.e)

LANES = 128
SUBLANES = 8
VMEM_LIMIT_BYTES = 56 * 1024 * 1024

TM_IN = 2048
IN_MIN_TILES = 4
TQ = 1024
TM_FF = 1024
CJ = 256
NJ = D_FF // CJ
N_SLAB = CONV_CH // LANES
SB = 128
CONV_ROWS = 32
G_HALO = 16
U_HALO = SUBLANES
N_VARIANT = 3


def _sigmoid(x):
    return 1.0 / (1.0 + jnp.exp2(x * -LOG2E))


def _rmsnorm(x, g):
    ms = jnp.mean(x * x, axis=-1, keepdims=True)
    return (x * lax.rsqrt(ms + EPS)) * g


def _resident(shape):
    return pl.BlockSpec(shape, lambda bi, i: (0,) * len(shape), pipeline_mode=pl.Buffered(1))


def _halo_specs(tm, halo, seq):
    per_tile = tm // halo
    n_halo = seq // halo
    return [
        pl.BlockSpec((None, tm, D_MODEL), lambda bi, i: (bi, i, 0)),
        pl.BlockSpec((None, halo, D_MODEL), lambda bi, i: (bi, jnp.maximum(i * per_tile - 1, 0), 0)),
        pl.BlockSpec((None, halo, D_MODEL), lambda bi, i: (bi, jnp.minimum((i + 1) * per_tile, n_halo - 1), 0)),
    ]


def _bias_table_kernel(rb_ref, bucket_ref, valid_ref, out_ref):
    bucket = bucket_ref[...]
    valid = valid_ref[...] != 0
    for n in range(N_HEADS):
        val = jnp.zeros(bucket.shape, F32)
        for b in range(N_BUCKETS):
            val = jnp.where(bucket == b, rb_ref[b, n], val)
        out_ref[n] = jnp.where(valid, val * LOG2E, NEG)


def _t5_buckets(rel):
    nb = N_BUCKETS // 2
    ret = (rel > 0).astype(jnp.int32) * nb
    n = jnp.abs(rel)
    max_exact = nb // 2
    nf = jnp.maximum(n, 1).astype(F32)
    large = max_exact + (jnp.log(nf / max_exact) / math.log(MAX_DISTANCE / max_exact)
                         * (nb - max_exact)).astype(jnp.int32)
    large = jnp.minimum(large, nb - 1)
    return ret + jnp.where(n < max_exact, n, large)


def _band_tables(rel_bias):
    qi = jnp.arange(BLOCK, dtype=jnp.int32)[:, None]
    kj = jnp.arange(BAND, dtype=jnp.int32)[None, :]
    rel = jnp.broadcast_to(kj - qi - BLOCK, (N_VARIANT, BLOCK, BAND))
    in_seq = jnp.stack([kj >= 0, kj >= BLOCK, kj < 2 * BLOCK])
    bucket = _t5_buckets(rel)
    valid = ((jnp.abs(rel) <= WINDOW) & in_seq).astype(jnp.int32)
    return pl.pallas_call(
        _bias_table_kernel,
        grid=(N_VARIANT,),
        in_specs=[
            pl.BlockSpec(memory_space=pltpu.SMEM),
            pl.BlockSpec((None, BLOCK, BAND), lambda v: (v, 0, 0)),
            pl.BlockSpec((None, BLOCK, BAND), lambda v: (v, 0, 0)),
        ],
        out_specs=pl.BlockSpec((None, N_HEADS, BLOCK, BAND), lambda v: (v, 0, 0, 0)),
        out_shape=jax.ShapeDtypeStruct((N_VARIANT, N_HEADS, BLOCK, BAND), F32),
        name="band_bias_tables",
    )(rel_bias.astype(F32), bucket, valid)


def _inproj_kernel(x_ref, xp_ref, xn_ref, g_ref, wqkv_ref, wc_ref, cw_ref, cb_ref, lng_ref, lnb_ref,
                   q_ref, k_ref, v_ref, c_ref, hb_ref, gl_ref, cv_ref, *, tm):
    i = pl.program_id(1)
    first = i == 0
    last = i == pl.num_programs(1) - 1
    g = g_ref[...]

    def norm_rows(lo, hi):
        if lo < G_HALO:
            hb_ref[lo:G_HALO, :] = jnp.where(first, 0.0, _rmsnorm(xp_ref[lo:, :], g)).astype(BF16)
        m_lo, m_hi = max(lo, G_HALO), min(hi, G_HALO + tm)
        hb_ref[m_lo:m_hi, :] = _rmsnorm(x_ref[m_lo - G_HALO:m_hi - G_HALO, :], g).astype(BF16)
        if hi > G_HALO + tm:
            hb_ref[G_HALO + tm:hi, :] = jnp.where(last, 0.0, _rmsnorm(xn_ref[:hi - G_HALO - tm, :], g)).astype(BF16)

    def glu_rows(lo, hi):
        zc = jnp.dot(hb_ref[lo:hi, :], wc_ref[...], preferred_element_type=F32)
        glu = zc[:, :CONV_CH] * _sigmoid(zc[:, CONV_CH:])
        for c in range(N_SLAB):
            gl_ref[c, lo:hi, :] = glu[:, c * LANES:(c + 1) * LANES]

    def qkv_rows(r):
        z = jnp.dot(hb_ref[G_HALO + r:G_HALO + r + SB, :], wqkv_ref[...], preferred_element_type=F32)
        q_ref[r:r + SB, :] = (z[:, :ATTN_WIDTH] * (SCALE * LOG2E)).astype(BF16)
        kf = z[:, ATTN_WIDTH:ATTN_WIDTH + KV_WIDTH]
        vf = z[:, ATTN_WIDTH + KV_WIDTH:]
        k_ref[r:r + SB, :KV_WIDTH] = kf.astype(BF16)
        k_ref[r:r + SB, KV_WIDTH:] = pltpu.roll(kf, HEAD_DIM, axis=1).astype(BF16)
        v_ref[r:r + SB, :KV_WIDTH] = vf.astype(BF16)
        v_ref[r:r + SB, KV_WIDTH:] = pltpu.roll(vf, HEAD_DIM, axis=1).astype(BF16)

    def conv_rows(r):
        for c in range(N_SLAB):
            lanes = slice(c * LANES, (c + 1) * LANES)
            for rr in range(r, r + SB, CONV_ROWS):
                acc = jnp.broadcast_to(cb_ref[:, lanes], (CONV_ROWS, LANES))
                for t in range(CONV_K):
                    lo = rr + (G_HALO - CONV_PAD) + t
                    acc = acc + gl_ref[c, lo:lo + CONV_ROWS, :] * cw_ref[t:t + 1, lanes]
                cv_ref[rr:rr + CONV_ROWS, lanes] = acc
        cv = cv_ref[r:r + SB, :]
        mu = jnp.mean(cv, axis=-1, keepdims=True)
        xc = cv - mu
        ln = (xc * lax.rsqrt(jnp.mean(xc * xc, axis=-1, keepdims=True) + EPS)) * lng_ref[...] + lnb_ref[...]
        c_ref[r:r + SB, :] = (ln * _sigmoid(ln)).astype(BF16)

    n_sb = tm // SB
    bounds = [0] + [(k + 1) * SB + 2 * G_HALO for k in range(n_sb)]
    norm_rows(bounds[0], bounds[1])
    glu_rows(bounds[0], bounds[1])
    for sb in range(n_sb):
        if sb + 1 < n_sb:
            norm_rows(bounds[sb + 1], bounds[sb + 2])
            glu_rows(bounds[sb + 1], bounds[sb + 2])
        qkv_rows(sb * SB)
        conv_rows(sb * SB)


def _inproj(x, g, wqkv, wc, cw, cb, lng, lnb):
    b, s, _ = x.shape
    tm = min(TM_IN, s // IN_MIN_TILES)
    return pl.pallas_call(
        functools.partial(_inproj_kernel, tm=tm),
        grid=(b, s // tm),
        in_specs=_halo_specs(tm, G_HALO, s) + [
            _resident((1, D_MODEL)),
            _resident((D_MODEL, QKV_WIDTH)),
            _resident((D_MODEL, 2 * CONV_CH)),
            _resident((CONV_K, CONV_CH)),
            _resident((1, CONV_CH)),
            _resident((1, CONV_CH)),
            _resident((1, CONV_CH)),
        ],
        out_specs=[
            pl.BlockSpec((None, tm, ATTN_WIDTH), lambda bi, i: (bi, i, 0)),
            pl.BlockSpec((None, tm, 2 * KV_WIDTH), lambda bi, i: (bi, i, 0)),
            pl.BlockSpec((None, tm, 2 * KV_WIDTH), lambda bi, i: (bi, i, 0)),
            pl.BlockSpec((None, tm, CONV_CH), lambda bi, i: (bi, i, 0)),
        ],
        out_shape=[
            jax.ShapeDtypeStruct((b, s, ATTN_WIDTH), BF16),
            jax.ShapeDtypeStruct((b, s, 2 * KV_WIDTH), BF16),
            jax.ShapeDtypeStruct((b, s, 2 * KV_WIDTH), BF16),
            jax.ShapeDtypeStruct((b, s, CONV_CH), BF16),
        ],
        scratch_shapes=[
            pltpu.VMEM((tm + 2 * G_HALO, D_MODEL), BF16),
            pltpu.VMEM((N_SLAB, tm + 2 * G_HALO, LANES), F32),
            pltpu.VMEM((tm, CONV_CH), F32),
        ],
        compiler_params=pltpu.CompilerParams(
            dimension_semantics=("arbitrary", "arbitrary"), vmem_limit_bytes=VMEM_LIMIT_BYTES),
        name="inproj",
    )(x, x, x, g, wqkv, wc, cw, cb, lng, lnb)


def _mixer_kernel(sink_ref, x_ref, q_ref, k_ref, v_ref, c_ref, bias_ref, woa_ref, woc_ref, out_ref,
                  kz_ref, vz_ref, y_ref, sa_ref, sb_ref, *, seq):
    i = pl.program_id(1)
    r0 = pl.multiple_of(i * TQ, TQ)
    n_blk = TQ // BLOCK

    left = pl.multiple_of(jnp.maximum(r0 - BLOCK, 0), BLOCK)
    right = pl.multiple_of(jnp.minimum(r0 + TQ, seq - BLOCK), BLOCK)
    parts = ((0, left, BLOCK), (BLOCK, r0, TQ), (BLOCK + TQ, right, BLOCK))
    lo = lax.broadcasted_iota(jnp.int32, (1, LANES), 1) < HEAD_DIM
    for src_ref, dst_ref, fill in ((k_ref, kz_ref, jnp.zeros((), BF16)), (v_ref, vz_ref, jnp.ones((), BF16))):
        for dst0, src0, rows in parts:
            w = src_ref[pl.ds(src0, rows), :]
            straight, swapped = w[:, :KV_WIDTH], w[:, KV_WIDTH:]
            dst = slice(dst0, dst0 + rows)
            dst_ref[0, dst, :] = jnp.where(lo, straight, fill)
            dst_ref[1, dst, :] = jnp.where(lo, fill, swapped)
            dst_ref[2, dst, :] = jnp.where(lo, swapped, fill)
            dst_ref[3, dst, :] = jnp.where(lo, fill, straight)

    dyn0 = pl.multiple_of(jnp.minimum(i, 0) * BLOCK, BLOCK)

    def block_pos(jb):
        r = r0 + jb * BLOCK
        var = jnp.where(r == 0, 1, jnp.where(r == seq - BLOCK, 2, 0))
        return jb * BLOCK, var

    def score_stage(jb, s_ref):
        koff, _ = block_pos(jb)
        row = jb * BLOCK
        for n in range(N_HEADS):
            m, e = divmod(n, 2)
            h = m // (GROUP // 2)
            qp = q_ref[row:row + BLOCK, m * LANES:(m + 1) * LANES]
            kz = kz_ref[2 * h + e, pl.ds(koff, BAND), :]
            s_ref[n] = lax.dot_general(qp, kz, (((1,), (1,)), ((), ())), preferred_element_type=F32)

    def value_stage(jb, s_ref):
        koff, var = block_pos(jb)
        row = jb * BLOCK
        for m in range(N_HEADS // 2):
            h = m // (GROUP // 2)
            pv = []
            sk = []
            for e in range(2):
                n = 2 * m + e
                s = s_ref[n, pl.ds(dyn0, BLOCK), :] + bias_ref[var, n]
                sink = sink_ref[n] * LOG2E
                mx = jnp.maximum(jnp.max(s, axis=-1, keepdims=True), sink)
                p = jnp.exp2(s - mx)
                sk.append(jnp.broadcast_to(jnp.exp2(sink - mx), (BLOCK, LANES)))
                vz = vz_ref[2 * h + e, pl.ds(koff, BAND), :]
                pv.append(jnp.dot(p.astype(BF16), vz, preferred_element_type=F32))
            outs = jnp.where(lo, pv[0], pv[1])
            sums = pltpu.roll(jnp.where(lo, pv[1], pv[0]), HEAD_DIM, axis=1)
            denom = sums + jnp.where(lo, sk[0], sk[1])
            y_ref[row:row + BLOCK, m * LANES:(m + 1) * LANES] = (outs / denom).astype(BF16)

    def out_proj(jb):
        rows = slice(jb * BLOCK, (jb + 1) * BLOCK)
        out_ref[rows, :] = (x_ref[rows, :]
                            + jnp.dot(y_ref[rows, :], woa_ref[...], preferred_element_type=F32)
                            + jnp.dot(c_ref[rows, :], woc_ref[...], preferred_element_type=F32))

    stage = (sa_ref, sb_ref)
    score_stage(0, stage[0])
    for jb in range(n_blk):
        if jb + 1 < n_blk:
            score_stage(jb + 1, stage[(jb + 1) % 2])
        if jb > 0:
            out_proj(jb - 1)
        value_stage(jb, stage[jb % 2])
    out_proj(n_blk - 1)


def _mixer(x, q, k, v, c_act, tables, sink, woa, woc):
    b, s, _ = x.shape
    kernel = functools.partial(_mixer_kernel, seq=s)
    return pl.pallas_call(
        kernel,
        grid=(b, s // TQ),
        in_specs=[
            pl.BlockSpec(memory_space=pltpu.SMEM),
            pl.BlockSpec((None, TQ, D_MODEL), lambda bi, i: (bi, i, 0)),
            pl.BlockSpec((None, TQ, ATTN_WIDTH), lambda bi, i: (bi, i, 0)),
            pl.BlockSpec((None, s, 2 * KV_WIDTH), lambda bi, i: (bi, 0, 0)),
            pl.BlockSpec((None, s, 2 * KV_WIDTH), lambda bi, i: (bi, 0, 0)),
            pl.BlockSpec((None, TQ, CONV_CH), lambda bi, i: (bi, i, 0)),
            _resident((N_VARIANT, N_HEADS, BLOCK, BAND)),
            _resident((ATTN_WIDTH, D_MODEL)),
            _resident((CONV_CH, D_MODEL)),
        ],
        out_specs=pl.BlockSpec((None, TQ, D_MODEL), lambda bi, i: (bi, i, 0)),
        out_shape=jax.ShapeDtypeStruct((b, s, D_MODEL), F32),
        scratch_shapes=[
            pltpu.VMEM((2 * N_KV, TQ + 2 * BLOCK, LANES), BF16),
            pltpu.VMEM((2 * N_KV, TQ + 2 * BLOCK, LANES), BF16),
            pltpu.VMEM((TQ, ATTN_WIDTH), BF16),
            pltpu.VMEM((N_HEADS, BLOCK, BAND), F32),
            pltpu.VMEM((N_HEADS, BLOCK, BAND), F32),
        ],
        compiler_params=pltpu.CompilerParams(
            dimension_semantics=("arbitrary", "arbitrary"), vmem_limit_bytes=VMEM_LIMIT_BYTES),
        name="mixer",
    )(sink, x, q, k, v, c_act, tables, woa, woc)


def _ffn_kernel(x_ref, xp_ref, xn_ref, g_ref, wup_ref, dw_ref, db_ref, wdn_ref, gf_ref, out_ref,
                hb_ref, uga_ref, uva_ref, ugb_ref, uvb_ref, *, final_norm):
    i = pl.program_id(1)
    first = i == 0
    last = i == pl.num_programs(1) - 1
    tm = TM_FF
    g = g_ref[...]
    hb_ref[:U_HALO, :] = jnp.where(first, 0.0, _rmsnorm(xp_ref[...], g)).astype(BF16)
    hb_ref[U_HALO:U_HALO + tm, :] = _rmsnorm(x_ref[...], g).astype(BF16)
    hb_ref[U_HALO + tm:, :] = jnp.where(last, 0.0, _rmsnorm(xn_ref[...], g)).astype(BF16)
    out_ref[...] = x_ref[...]

    def up_proj(j, ug_ref, uv_ref):
        hb = hb_ref[...]
        ug = jnp.dot(hb, wup_ref[0, j], preferred_element_type=F32)
        uv = jnp.dot(hb, wup_ref[1, j], preferred_element_type=F32)
        for c in range(CJ // LANES):
            ug_ref[c] = ug[:, c * LANES:(c + 1) * LANES]
            uv_ref[c] = uv[:, c * LANES:(c + 1) * LANES]

    def conv3(u_ref, c, w, bias):
        lanes = slice(c * LANES, (c + 1) * LANES)
        out = bias[:, lanes] + u_ref[c, U_HALO - 1:U_HALO - 1 + tm, :] * w[0:1, lanes]
        out = out + u_ref[c, U_HALO:U_HALO + tm, :] * w[1:2, lanes]
        return out + u_ref[c, U_HALO + 1:U_HALO + 1 + tm, :] * w[2:3, lanes]

    def down_proj(j, ug_ref, uv_ref):
        acts = []
        for c in range(CJ // LANES):
            gate = conv3(ug_ref, c, dw_ref[0, j], db_ref[0, j])
            val = conv3(uv_ref, c, dw_ref[1, j], db_ref[1, j])
            acts.append(((gate * _sigmoid(gate)) * val).astype(BF16))
        act = jnp.concatenate(acts, axis=1)
        out_ref[...] += jnp.dot(act, wdn_ref[j], preferred_element_type=F32)

    up_proj(0, uga_ref, uva_ref)

    def chunk_pair(j):
        up_proj(j + 1, ugb_ref, uvb_ref)
        down_proj(j, uga_ref, uva_ref)
        up_proj(j + 2, uga_ref, uva_ref)
        down_proj(j + 1, ugb_ref, uvb_ref)

    def chunk_quad(p, carry):
        chunk_pair(4 * p)
        chunk_pair(4 * p + 2)
        return carry

    n_quad = (NJ - 1) // 4
    lax.fori_loop(0, n_quad, chunk_quad, 0)
    for j in range(4 * n_quad, NJ - 1, 2):
        chunk_pair(j)
    down_proj(NJ - 1, uga_ref, uva_ref)
    if final_norm:
        out_ref[...] = _rmsnorm(out_ref[...], gf_ref[...])


def _ffn(x, g, wup, dw, db, wdn, gf, final_norm):
    b, s, _ = x.shape
    tm = TM_FF
    kernel = functools.partial(_ffn_kernel, final_norm=final_norm)
    return pl.pallas_call(
        kernel,
        grid=(b, s // tm),
        in_specs=_halo_specs(tm, U_HALO, s) + [
            _resident((1, D_MODEL)),
            _resident((2, NJ, D_MODEL, CJ)),
            _resident((2, NJ, FFN_CONV_K, CJ)),
            _resident((2, NJ, 1, CJ)),
            _resident((NJ, CJ, D_MODEL)),
            _resident((1, D_MODEL)),
        ],
        out_specs=pl.BlockSpec((None, tm, D_MODEL), lambda bi, i: (bi, i, 0)),
        out_shape=jax.ShapeDtypeStruct((b, s, D_MODEL), F32),
        scratch_shapes=[pltpu.VMEM((tm + 2 * U_HALO, D_MODEL), BF16)]
        + [pltpu.VMEM((CJ // LANES, tm + 2 * U_HALO, LANES), F32)] * 4,
        compiler_params=pltpu.CompilerParams(
            dimension_semantics=("arbitrary", "arbitrary"), vmem_limit_bytes=VMEM_LIMIT_BYTES),
        name="ffn",
    )(x, x, x, g, wup, dw, db, wdn, gf)


def _split_ff(a):
    lead = a.shape[:-1]
    a = a.reshape(lead + (2, NJ, CJ))
    return jnp.moveaxis(a, (-3, -2), (0, 1))


def _prepare_layer(l, norm_attn_g, w_in, attn_sink, conv_dw_w, conv_dw_b, conv_ln_g, conv_ln_b, w_out,
                   norm_ffn_g, w_up, ffn_dw_w, ffn_dw_b, w_down):
    return dict(
        g_attn=norm_attn_g[l].astype(F32)[None, :],
        wqkv=w_in[l][:, :QKV_WIDTH].astype(BF16),
        wc=w_in[l][:, QKV_WIDTH:].astype(BF16),
        sink=attn_sink[l].astype(F32),
        cw=conv_dw_w[l].astype(F32),
        cb=conv_dw_b[l].astype(F32)[None, :],
        lng=conv_ln_g[l].astype(F32)[None, :],
        lnb=conv_ln_b[l].astype(F32)[None, :],
        woa=w_out[l][:ATTN_WIDTH].astype(BF16),
        woc=w_out[l][ATTN_WIDTH:].astype(BF16),
        g_ffn=norm_ffn_g[l].astype(F32)[None, :],
        wup=_split_ff(w_up[l].astype(BF16)),
        dw=_split_ff(ffn_dw_w[l].astype(F32)),
        db=_split_ff(ffn_dw_b[l].astype(F32)[None, :]),
        wdn=w_down[l].astype(BF16).reshape(NJ, CJ, D_MODEL),
    )


def _trunk(x, tables, layers, gf):
    depth = len(layers)
    for l, p in enumerate(layers):
        q, k, v, c_act = _inproj(x, p["g_attn"], p["wqkv"], p["wc"], p["cw"], p["cb"], p["lng"], p["lnb"])
        x = _mixer(x, q, k, v, c_act, tables, p["sink"], p["woa"], p["woc"])
        x = _ffn(x, p["g_ffn"], p["wup"], p["dw"], p["db"], p["wdn"], gf, final_norm=(l == depth - 1))
    return x


def kernel(x_prompt, x_sample, rel_bias, norm_attn_g, w_in, attn_sink, conv_dw_w, conv_dw_b, conv_ln_g,
           conv_ln_b, w_out, norm_ffn_g, w_up, ffn_dw_w, ffn_dw_b, w_down, norm_final_g):
    depth = w_in.shape[0]
    tables = _band_tables(rel_bias)
    layers = [
        _prepare_layer(l, norm_attn_g, w_in, attn_sink, conv_dw_w, conv_dw_b, conv_ln_g, conv_ln_b, w_out,
                       norm_ffn_g, w_up, ffn_dw_w, ffn_dw_b, w_down)
        for l in range(depth)
    ]
    gf = norm_final_g.astype(F32)[None, :]
    return (_trunk(x_prompt, tables, layers, gf), _trunk(x_sample, tables, layers, gf))
```

```python
import functools
import math

import jax
import jax.numpy as jnp
from jax import lax
from jax.experimental import pallas as pl
from jax.experimental.pallas import tpu as pltpu

F32 = jnp.float32
BF16 = jnp.bfloat16

D_MODEL = 1024
HEAD_DIM = 64
N_HEADS = 8
N_KV = 2
GROUP = N_HEADS // N_KV
ATTN_WIDTH = N_HEADS * HEAD_DIM
KV_WIDTH = N_KV * HEAD_DIM
CONV_CH = D_MODEL - ATTN_WIDTH
CONV_K = 31
CONV_PAD = (CONV_K - 1) // 2
WINDOW = 128
BLOCK = 128
BAND = 3 * BLOCK
N_BUCKETS = 32
MAX_DISTANCE = 128
D_FF = 2816
FFN_CONV_K = 3
QKV_WIDTH = ATTN_WIDTH + 2 * KV_WIDTH
EPS = 1e-6
NEG = -1e30
SCALE = HEAD_DIM ** -0.5
LOG2E = math.log2(math.e)

LANES = 128
SUBLANES = 8
VMEM_LIMIT_BYTES = 56 * 1024 * 1024

TM_IN = 2048
IN_MIN_TILES = 4
TQ = 1024
TM_FF = 1024
CJ = 256
NJ = D_FF // CJ
N_SLAB = CONV_CH // LANES
SB = 128
CONV_ROWS = 16
G_HALO = 16
U_HALO = SUBLANES
N_VARIANT = 3


def _sigmoid(x):
    return 1.0 / (1.0 + jnp.exp2(x * -LOG2E))


def _rmsnorm(x, g):
    ms = jnp.mean(x * x, axis=-1, keepdims=True)
    return (x * lax.rsqrt(ms + EPS)) * g


def _resident(shape):
    return pl.BlockSpec(shape, lambda bi, i: (0,) * len(shape), pipeline_mode=pl.Buffered(1))


def _halo_specs(tm, halo, seq):
    per_tile = tm // halo
    n_halo = seq // halo
    return [
        pl.BlockSpec((None, tm, D_MODEL), lambda bi, i: (bi, i, 0)),
        pl.BlockSpec((None, halo, D_MODEL), lambda bi, i: (bi, jnp.maximum(i * per_tile - 1, 0), 0)),
        pl.BlockSpec((None, halo, D_MODEL), lambda bi, i: (bi, jnp.minimum((i + 1) * per_tile, n_halo - 1), 0)),
    ]


def _bias_table_kernel(rb_ref, bucket_ref, valid_ref, out_ref):
    bucket = bucket_ref[...]
    valid = valid_ref[...] != 0
    for n in range(N_HEADS):
        val = jnp.zeros(bucket.shape, F32)
        for b in range(N_BUCKETS):
            val = jnp.where(bucket == b, rb_ref[b, n], val)
        out_ref[n] = jnp.where(valid, val * LOG2E, NEG)


def _t5_buckets(rel):
    nb = N_BUCKETS // 2
    ret = (rel > 0).astype(jnp.int32) * nb
    n = jnp.abs(rel)
    max_exact = nb // 2
    nf = jnp.maximum(n, 1).astype(F32)
    large = max_exact + (jnp.log(nf / max_exact) / math.log(MAX_DISTANCE / max_exact)
                         * (nb - max_exact)).astype(jnp.int32)
    large = jnp.minimum(large, nb - 1)
    return ret + jnp.where(n < max_exact, n, large)


def _band_tables(rel_bias):
    qi = jnp.arange(BLOCK, dtype=jnp.int32)[:, None]
    kj = jnp.arange(BAND, dtype=jnp.int32)[None, :]
    rel = jnp.broadcast_to(kj - qi - BLOCK, (N_VARIANT, BLOCK, BAND))
    in_seq = jnp.stack([kj >= 0, kj >= BLOCK, kj < 2 * BLOCK])
    bucket = _t5_buckets(rel)
    valid = ((jnp.abs(rel) <= WINDOW) & in_seq).astype(jnp.int32)
    return pl.pallas_call(
        _bias_table_kernel,
        grid=(N_VARIANT,),
        in_specs=[
            pl.BlockSpec(memory_space=pltpu.SMEM),
            pl.BlockSpec((None, BLOCK, BAND), lambda v: (v, 0, 0)),
            pl.BlockSpec((None, BLOCK, BAND), lambda v: (v, 0, 0)),
        ],
        out_specs=pl.BlockSpec((None, N_HEADS, BLOCK, BAND), lambda v: (v, 0, 0, 0)),
        out_shape=jax.ShapeDtypeStruct((N_VARIANT, N_HEADS, BLOCK, BAND), F32),
        name="band_bias_tables",
    )(rel_bias.astype(F32), bucket, valid)


def _inproj_kernel(x_ref, xp_ref, xn_ref, g_ref, wqkv_ref, wc_ref, cw_ref, cb_ref, lng_ref, lnb_ref,
                   q_ref, k_ref, v_ref, c_ref, hb_ref, gl_ref, cv_ref, *, tm):
    i = pl.program_id(1)
    first = i == 0
    last = i == pl.num_programs(1) - 1
    g = g_ref[...]

    def norm_rows(lo, hi):
        if lo < G_HALO:
            hb_ref[lo:G_HALO, :] = jnp.where(first, 0.0, _rmsnorm(xp_ref[lo:, :], g)).astype(BF16)
        m_lo, m_hi = max(lo, G_HALO), min(hi, G_HALO + tm)
        hb_ref[m_lo:m_hi, :] = _rmsnorm(x_ref[m_lo - G_HALO:m_hi - G_HALO, :], g).astype(BF16)
        if hi > G_HALO + tm:
            hb_ref[G_HALO + tm:hi, :] = jnp.where(last, 0.0, _rmsnorm(xn_ref[:hi - G_HALO - tm, :], g)).astype(BF16)

    def glu_rows(lo, hi):
        zc = jnp.dot(hb_ref[lo:hi, :], wc_ref[...], preferred_element_type=F32)
        glu = zc[:, :CONV_CH] * _sigmoid(zc[:, CONV_CH:])
        for c in range(N_SLAB):
            gl_ref[c, lo:hi, :] = glu[:, c * LANES:(c + 1) * LANES]

    def qkv_rows(r):
        z = jnp.dot(hb_ref[G_HALO + r:G_HALO + r + SB, :], wqkv_ref[...], preferred_element_type=F32)
        q_ref[r:r + SB, :] = (z[:, :ATTN_WIDTH] * (SCALE * LOG2E)).astype(BF16)
        kf = z[:, ATTN_WIDTH:ATTN_WIDTH + KV_WIDTH]
        vf = z[:, ATTN_WIDTH + KV_WIDTH:]
        k_ref[r:r + SB, :KV_WIDTH] = kf.astype(BF16)
        k_ref[r:r + SB, KV_WIDTH:] = pltpu.roll(kf, HEAD_DIM, axis=1).astype(BF16)
        v_ref[r:r + SB, :KV_WIDTH] = vf.astype(BF16)
        v_ref[r:r + SB, KV_WIDTH:] = pltpu.roll(vf, HEAD_DIM, axis=1).astype(BF16)

    def conv_rows(r):
        for c in range(N_SLAB):
            lanes = slice(c * LANES, (c + 1) * LANES)
            for rr in range(r, r + SB, CONV_ROWS):
                acc = jnp.broadcast_to(cb_ref[:, lanes], (CONV_ROWS, LANES))
                for t in range(CONV_K):
                    lo = rr + (G_HALO - CONV_PAD) + t
                    acc = acc + gl_ref[c, lo:lo + CONV_ROWS, :] * cw_ref[t:t + 1, lanes]
                cv_ref[rr:rr + CONV_ROWS, lanes] = acc
        cv = cv_ref[r:r + SB, :]
        mu = jnp.mean(cv, axis=-1, keepdims=True)
        xc = cv - mu
        ln = (xc * lax.rsqrt(jnp.mean(xc * xc, axis=-1, keepdims=True) + EPS)) * lng_ref[...] + lnb_ref[...]
        c_ref[r:r + SB, :] = (ln * _sigmoid(ln)).astype(BF16)

    n_sb = tm // SB
    bounds = [0] + [(k + 1) * SB + 2 * G_HALO for k in range(n_sb)]
    norm_rows(bounds[0], bounds[1])
    glu_rows(bounds[0], bounds[1])
    for sb in range(n_sb):
        if sb + 1 < n_sb:
            norm_rows(bounds[sb + 1], bounds[sb + 2])
            glu_rows(bounds[sb + 1], bounds[sb + 2])
        qkv_rows(sb * SB)
        conv_rows(sb * SB)


def _inproj(x, g, wqkv, wc, cw, cb, lng, lnb):
    b, s, _ = x.shape
    tm = min(TM_IN, s // IN_MIN_TILES)
    return pl.pallas_call(
        functools.partial(_inproj_kernel, tm=tm),
        grid=(b, s // tm),
        in_specs=_halo_specs(tm, G_HALO, s) + [
            _resident((1, D_MODEL)),
            _resident((D_MODEL, QKV_WIDTH)),
            _resident((D_MODEL, 2 * CONV_CH)),
            _resident((CONV_K, CONV_CH)),
            _resident((1, CONV_CH)),
            _resident((1, CONV_CH)),
            _resident((1, CONV_CH)),
        ],
        out_specs=[
            pl.BlockSpec((None, tm, ATTN_WIDTH), lambda bi, i: (bi, i, 0)),
            pl.BlockSpec((None, tm, 2 * KV_WIDTH), lambda bi, i: (bi, i, 0)),
            pl.BlockSpec((None, tm, 2 * KV_WIDTH), lambda bi, i: (bi, i, 0)),
            pl.BlockSpec((None, tm, CONV_CH), lambda bi, i: (bi, i, 0)),
        ],
        out_shape=[
            jax.ShapeDtypeStruct((b, s, ATTN_WIDTH), BF16),
            jax.ShapeDtypeStruct((b, s, 2 * KV_WIDTH), BF16),
            jax.ShapeDtypeStruct((b, s, 2 * KV_WIDTH), BF16),
            jax.ShapeDtypeStruct((b, s, CONV_CH), BF16),
        ],
        scratch_shapes=[
            pltpu.VMEM((tm + 2 * G_HALO, D_MODEL), BF16),
            pltpu.VMEM((N_SLAB, tm + 2 * G_HALO, LANES), F32),
            pltpu.VMEM((tm, CONV_CH), F32),
        ],
        compiler_params=pltpu.CompilerParams(
            dimension_semantics=("arbitrary", "arbitrary"), vmem_limit_bytes=VMEM_LIMIT_BYTES),
        name="inproj",
    )(x, x, x, g, wqkv, wc, cw, cb, lng, lnb)


def _mixer_kernel(sink_ref, x_ref, q_ref, k_ref, v_ref, c_ref, bias_ref, woa_ref, woc_ref, out_ref,
                  kz_ref, vz_ref, y_ref, sa_ref, sb_ref, *, seq):
    i = pl.program_id(1)
    r0 = pl.multiple_of(i * TQ, TQ)
    n_blk = TQ // BLOCK

    left = pl.multiple_of(jnp.maximum(r0 - BLOCK, 0), BLOCK)
    right = pl.multiple_of(jnp.minimum(r0 + TQ, seq - BLOCK), BLOCK)
    parts = ((0, left, BLOCK), (BLOCK, r0, TQ), (BLOCK + TQ, right, BLOCK))
    lo = lax.broadcasted_iota(jnp.int32, (1, LANES), 1) < HEAD_DIM
    for src_ref, dst_ref, fill in ((k_ref, kz_ref, jnp.zeros((), BF16)), (v_ref, vz_ref, jnp.ones((), BF16))):
        for dst0, src0, rows in parts:
            w = src_ref[pl.ds(src0, rows), :]
            straight, swapped = w[:, :KV_WIDTH], w[:, KV_WIDTH:]
            dst = slice(dst0, dst0 + rows)
            dst_ref[0, dst, :] = jnp.where(lo, straight, fill)
            dst_ref[1, dst, :] = jnp.where(lo, fill, swapped)
            dst_ref[2, dst, :] = jnp.where(lo, swapped, fill)
            dst_ref[3, dst, :] = jnp.where(lo, fill, straight)

    dyn0 = pl.multiple_of(jnp.minimum(i, 0) * BLOCK, BLOCK)

    def block_pos(jb):
        r = r0 + jb * BLOCK
        var = jnp.where(r == 0, 1, jnp.where(r == seq - BLOCK, 2, 0))
        return jb * BLOCK, var

    def score_stage(jb, s_ref):
        koff, _ = block_pos(jb)
        row = jb * BLOCK
        for n in range(N_HEADS):
            m, e = divmod(n, 2)
            h = m // (GROUP // 2)
            qp = q_ref[row:row + BLOCK, m * LANES:(m + 1) * LANES]
            kz = kz_ref[2 * h + e, pl.ds(koff, BAND), :]
            s_ref[n] = lax.dot_general(qp, kz, (((1,), (1,)), ((), ())), preferred_element_type=F32)

    def value_stage(jb, s_ref):
        koff, var = block_pos(jb)
        row = jb * BLOCK
        for m in range(N_HEADS // 2):
            h = m // (GROUP // 2)
            pv = []
            sk = []
            for e in range(2):
                n = 2 * m + e
                s = s_ref[n, pl.ds(dyn0, BLOCK), :] + bias_ref[var, n]
                sink = sink_ref[n] * LOG2E
                mx = jnp.maximum(jnp.max(s, axis=-1, keepdims=True), sink)
                p = jnp.exp2(s - mx)
                sk.append(jnp.broadcast_to(jnp.exp2(sink - mx), (BLOCK, LANES)))
                vz = vz_ref[2 * h + e, pl.ds(koff, BAND), :]
                pv.append(jnp.dot(p.astype(BF16), vz, preferred_element_type=F32))
            outs = jnp.where(lo, pv[0], pv[1])
            sums = pltpu.roll(jnp.where(lo, pv[1], pv[0]), HEAD_DIM, axis=1)
            denom = sums + jnp.where(lo, sk[0], sk[1])
            y_ref[row:row + BLOCK, m * LANES:(m + 1) * LANES] = (outs / denom).astype(BF16)

    def out_proj(jb):
        rows = slice(jb * BLOCK, (jb + 1) * BLOCK)
        out_ref[rows, :] = (x_ref[rows, :]
                            + jnp.dot(y_ref[rows, :], woa_ref[...], preferred_element_type=F32)
                            + jnp.dot(c_ref[rows, :], woc_ref[...], preferred_element_type=F32))

    stage = (sa_ref, sb_ref)
    score_stage(0, stage[0])
    for jb in range(n_blk):
        if jb + 1 < n_blk:
            score_stage(jb + 1, stage[(jb + 1) % 2])
        if jb > 0:
            out_proj(jb - 1)
        value_stage(jb, stage[jb % 2])
    out_proj(n_blk - 1)


def _mixer(x, q, k, v, c_act, tables, sink, woa, woc):
    b, s, _ = x.shape
    kernel = functools.partial(_mixer_kernel, seq=s)
    return pl.pallas_call(
        kernel,
        grid=(b, s // TQ),
        in_specs=[
            pl.BlockSpec(memory_space=pltpu.SMEM),
            pl.BlockSpec((None, TQ, D_MODEL), lambda bi, i: (bi, i, 0)),
            pl.BlockSpec((None, TQ, ATTN_WIDTH), lambda bi, i: (bi, i, 0)),
            pl.BlockSpec((None, s, 2 * KV_WIDTH), lambda bi, i: (bi, 0, 0)),
            pl.BlockSpec((None, s, 2 * KV_WIDTH), lambda bi, i: (bi, 0, 0)),
            pl.BlockSpec((None, TQ, CONV_CH), lambda bi, i: (bi, i, 0)),
            _resident((N_VARIANT, N_HEADS, BLOCK, BAND)),
            _resident((ATTN_WIDTH, D_MODEL)),
            _resident((CONV_CH, D_MODEL)),
        ],
        out_specs=pl.BlockSpec((None, TQ, D_MODEL), lambda bi, i: (bi, i, 0)),
        out_shape=jax.ShapeDtypeStruct((b, s, D_MODEL), F32),
        scratch_shapes=[
            pltpu.VMEM((2 * N_KV, TQ + 2 * BLOCK, LANES), BF16),
            pltpu.VMEM((2 * N_KV, TQ + 2 * BLOCK, LANES), BF16),
            pltpu.VMEM((TQ, ATTN_WIDTH), BF16),
            pltpu.VMEM((N_HEADS, BLOCK, BAND), F32),
            pltpu.VMEM((N_HEADS, BLOCK, BAND), F32),
        ],
        compiler_params=pltpu.CompilerParams(
            dimension_semantics=("arbitrary", "arbitrary"), vmem_limit_bytes=VMEM_LIMIT_BYTES),
        name="mixer",
    )(sink, x, q, k, v, c_act, tables, woa, woc)


def _ffn_kernel(x_ref, xp_ref, xn_ref, g_ref, wup_ref, dw_ref, db_ref, wdn_ref, gf_ref, out_ref,
                hb_ref, uga_ref, uva_ref, ugb_ref, uvb_ref, *, final_norm):
    i = pl.program_id(1)
    first = i == 0
    last = i == pl.num_programs(1) - 1
    tm = TM_FF
    g = g_ref[...]
    hb_ref[:U_HALO, :] = jnp.where(first, 0.0, _rmsnorm(xp_ref[...], g)).astype(BF16)
    hb_ref[U_HALO:U_HALO + tm, :] = _rmsnorm(x_ref[...], g).astype(BF16)
    hb_ref[U_HALO + tm:, :] = jnp.where(last, 0.0, _rmsnorm(xn_ref[...], g)).astype(BF16)
    out_ref[...] = x_ref[...]

    def up_proj(j, ug_ref, uv_ref):
        hb = hb_ref[...]
        ug = jnp.dot(hb, wup_ref[0, j], preferred_element_type=F32)
        uv = jnp.dot(hb, wup_ref[1, j], preferred_element_type=F32)
        for c in range(CJ // LANES):
            ug_ref[c] = ug[:, c * LANES:(c + 1) * LANES]
            uv_ref[c] = uv[:, c * LANES:(c + 1) * LANES]

    def conv3(u_ref, c, w, bias):
        lanes = slice(c * LANES, (c + 1) * LANES)
        out = bias[:, lanes] + u_ref[c, U_HALO - 1:U_HALO - 1 + tm, :] * w[0:1, lanes]
        out = out + u_ref[c, U_HALO:U_HALO + tm, :] * w[1:2, lanes]
        return out + u_ref[c, U_HALO + 1:U_HALO + 1 + tm, :] * w[2:3, lanes]

    def down_proj(j, ug_ref, uv_ref):
        acts = []
        for c in range(CJ // LANES):
            gate = conv3(ug_ref, c, dw_ref[0, j], db_ref[0, j])
            val = conv3(uv_ref, c, dw_ref[1, j], db_ref[1, j])
            acts.append(((gate * _sigmoid(gate)) * val).astype(BF16))
        act = jnp.concatenate(acts, axis=1)
        out_ref[...] += jnp.dot(act, wdn_ref[j], preferred_element_type=F32)

    up_proj(0, uga_ref, uva_ref)

    def chunk_pair(j):
        up_proj(j + 1, ugb_ref, uvb_ref)
        down_proj(j, uga_ref, uva_ref)
        up_proj(j + 2, uga_ref, uva_ref)
        down_proj(j + 1, ugb_ref, uvb_ref)

    def chunk_quad(p, carry):
        chunk_pair(4 * p)
        chunk_pair(4 * p + 2)
        return carry

    n_quad = (NJ - 1) // 4
    lax.fori_loop(0, n_quad, chunk_quad, 0)
    for j in range(4 * n_quad, NJ - 1, 2):
        chunk_pair(j)
    down_proj(NJ - 1, uga_ref, uva_ref)
    if final_norm:
        out_ref[...] = _rmsnorm(out_ref[...], gf_ref[...])


def _ffn(x, g, wup, dw, db, wdn, gf, final_norm):
    b, s, _ = x.shape
    tm = TM_FF
    kernel = functools.partial(_ffn_kernel, final_norm=final_norm)
    return pl.pallas_call(
        kernel,
        grid=(b, s // tm),
        in_specs=_halo_specs(tm, U_HALO, s) + [
            _resident((1, D_MODEL)),
            _resident((2, NJ, D_MODEL, CJ)),
            _resident((2, NJ, FFN_CONV_K, CJ)),
            _resident((2, NJ, 1, CJ)),
            _resident((NJ, CJ, D_MODEL)),
            _resident((1, D_MODEL)),
        ],
        out_specs=pl.BlockSpec((None, tm, D_MODEL), lambda bi, i: (bi, i, 0)),
        out_shape=jax.ShapeDtypeStruct((b, s, D_MODEL), F32),
        scratch_shapes=[pltpu.VMEM((tm + 2 * U_HALO, D_MODEL), BF16)]
        + [pltpu.VMEM((CJ // LANES, tm + 2 * U_HALO, LANES), F32)] * 4,
        compiler_params=pltpu.CompilerParams(
            dimension_semantics=("arbitrary", "arbitrary"), vmem_limit_bytes=VMEM_LIMIT_BYTES),
        name="ffn",
    )(x, x, x, g, wup, dw, db, wdn, gf)


def _split_ff(a):
    lead = a.shape[:-1]
    a = a.reshape(lead + (2, NJ, CJ))
    return jnp.moveaxis(a, (-3, -2), (0, 1))


def _prepare_layer(l, norm_attn_g, w_in, attn_sink, conv_dw_w, conv_dw_b, conv_ln_g, conv_ln_b, w_out,
                   norm_ffn_g, w_up, ffn_dw_w, ffn_dw_b, w_down):
    return dict(
        g_attn=norm_attn_g[l].astype(F32)[None, :],
        wqkv=w_in[l][:, :QKV_WIDTH].astype(BF16),
        wc=w_in[l][:, QKV_WIDTH:].astype(BF16),
        sink=attn_sink[l].astype(F32),
        cw=conv_dw_w[l].astype(F32),
        cb=conv_dw_b[l].astype(F32)[None, :],
        lng=conv_ln_g[l].astype(F32)[None, :],
        lnb=conv_ln_b[l].astype(F32)[None, :],
        woa=w_out[l][:ATTN_WIDTH].astype(BF16),
        woc=w_out[l][ATTN_WIDTH:].astype(BF16),
        g_ffn=norm_ffn_g[l].astype(F32)[None, :],
        wup=_split_ff(w_up[l].astype(BF16)),
        dw=_split_ff(ffn_dw_w[l].astype(F32)),
        db=_split_ff(ffn_dw_b[l].astype(F32)[None, :]),
        wdn=w_down[l].astype(BF16).reshape(NJ, CJ, D_MODEL),
    )


def _trunk(x, tables, layers, gf):
    depth = len(layers)
    for l, p in enumerate(layers):
        q, k, v, c_act = _inproj(x, p["g_attn"], p["wqkv"], p["wc"], p["cw"], p["cb"], p["lng"], p["lnb"])
        x = _mixer(x, q, k, v, c_act, tables, p["sink"], p["woa"], p["woc"])
        x = _ffn(x, p["g_ffn"], p["wup"], p["dw"], p["db"], p["wdn"], gf, final_norm=(l == depth - 1))
    return x


def kernel(x_prompt, x_sample, rel_bias, norm_attn_g, w_in, attn_sink, conv_dw_w, conv_dw_b, conv_ln_g,
           conv_ln_b, w_out, norm_ffn_g, w_up, ffn_dw_w, ffn_dw_b, w_down, norm_final_g):
    depth = w_in.shape[0]
    tables = _band_tables(rel_bias)
    layers = [
        _prepare_layer(l, norm_attn_g, w_in, attn_sink, conv_dw_w, conv_dw_b, conv_ln_g, conv_ln_b, w_out,
                       norm_ffn_g, w_up, ffn_dw_w, ffn_dw_b, w_down)
        for l in range(depth)
    ]
    gf = norm_final_g.astype(F32)[None, :]
    return (_trunk(x_prompt, tables, layers, gf), _trunk(x_sample, tables, layers, gf))
```

```python
import functools
import math

import jax
import jax.numpy as jnp
from jax import lax
from jax.experimental import pallas as pl
from jax.experimental.pallas import tpu as pltpu

F32 = jnp.float32
BF16 = jnp.bfloat16

D_MODEL = 1024
HEAD_DIM = 64
N_HEADS = 8
N_KV = 2
GROUP = N_HEADS // N_KV
ATTN_WIDTH = N_HEADS * HEAD_DIM
KV_WIDTH = N_KV * HEAD_DIM
CONV_CH = D_MODEL - ATTN_WIDTH
CONV_K = 31
CONV_PAD = (CONV_K - 1) // 2
WINDOW = 128
BLOCK = 128
BAND = 3 * BLOCK
N_BUCKETS = 32
MAX_DISTANCE = 128
D_FF = 2816
FFN_CONV_K = 3
QKV_WIDTH = ATTN_WIDTH + 2 * KV_WIDTH
EPS = 1e-6
NEG = -1e30
SCALE = HEAD_DIM ** -0.5
LOG2E = math.log2(math.e)

LANES = 128
SUBLANES = 8
VMEM_LIMIT_BYTES = 56 * 1024 * 1024

TM_IN = 2048
IN_MIN_TILES = 4
TQ = 1024
TM_FF = 1024
CJ = 256
NJ = D_FF // CJ
N_SLAB = CONV_CH // LANES
SB = 128
CONV_ROWS = 32
G_HALO = 16
U_HALO = SUBLANES
N_VARIANT = 3


def _sigmoid(x):
    return 1.0 / (1.0 + jnp.exp2(x * -LOG2E))


def _rmsnorm(x, g):
    ms = jnp.mean(x * x, axis=-1, keepdims=True)
    return (x * lax.rsqrt(ms + EPS)) * g


def _resident(shape):
    return pl.BlockSpec(shape, lambda bi, i: (0,) * len(shape), pipeline_mode=pl.Buffered(1))


def _halo_specs(tm, halo, seq):
    per_tile = tm // halo
    n_halo = seq // halo
    return [
        pl.BlockSpec((None, tm, D_MODEL), lambda bi, i: (bi, i, 0)),
        pl.BlockSpec((None, halo, D_MODEL), lambda bi, i: (bi, jnp.maximum(i * per_tile - 1, 0), 0)),
        pl.BlockSpec((None, halo, D_MODEL), lambda bi, i: (bi, jnp.minimum((i + 1) * per_tile, n_halo - 1), 0)),
    ]


def _bias_table_kernel(rb_ref, bucket_ref, valid_ref, out_ref):
    bucket = bucket_ref[...]
    valid = valid_ref[...] != 0
    for n in range(N_HEADS):
        val = jnp.zeros(bucket.shape, F32)
        for b in range(N_BUCKETS):
            val = jnp.where(bucket == b, rb_ref[b, n], val)
        out_ref[n] = jnp.where(valid, val * LOG2E, NEG)


def _t5_buckets(rel):
    nb = N_BUCKETS // 2
    ret = (rel > 0).astype(jnp.int32) * nb
    n = jnp.abs(rel)
    max_exact = nb // 2
    nf = jnp.maximum(n, 1).astype(F32)
    large = max_exact + (jnp.log(nf / max_exact) / math.log(MAX_DISTANCE / max_exact)
                         * (nb - max_exact)).astype(jnp.int32)
    large = jnp.minimum(large, nb - 1)
    return ret + jnp.where(n < max_exact, n, large)


def _band_tables(rel_bias):
    qi = jnp.arange(BLOCK, dtype=jnp.int32)[:, None]
    kj = jnp.arange(BAND, dtype=jnp.int32)[None, :]
    rel = jnp.broadcast_to(kj - qi - BLOCK, (N_VARIANT, BLOCK, BAND))
    in_seq = jnp.stack([kj >= 0, kj >= BLOCK, kj < 2 * BLOCK])
    bucket = _t5_buckets(rel)
    valid = ((jnp.abs(rel) <= WINDOW) & in_seq).astype(jnp.int32)
    return pl.pallas_call(
        _bias_table_kernel,
        grid=(N_VARIANT,),
        in_specs=[
            pl.BlockSpec(memory_space=pltpu.SMEM),
            pl.BlockSpec((None, BLOCK, BAND), lambda v: (v, 0, 0)),
            pl.BlockSpec((None, BLOCK, BAND), lambda v: (v, 0, 0)),
        ],
        out_specs=pl.BlockSpec((None, N_HEADS, BLOCK, BAND), lambda v: (v, 0, 0, 0)),
        out_shape=jax.ShapeDtypeStruct((N_VARIANT, N_HEADS, BLOCK, BAND), F32),
        name="band_bias_tables",
    )(rel_bias.astype(F32), bucket, valid)


def _inproj_kernel(x_ref, xp_ref, xn_ref, g_ref, wqkv_ref, wc_ref, cw_ref, cb_ref, lng_ref, lnb_ref,
                   q_ref, k_ref, v_ref, c_ref, hb_ref, gl_ref, cv_ref, *, tm):
    i = pl.program_id(1)
    first = i == 0
    last = i == pl.num_programs(1) - 1
    g = g_ref[...]

    def norm_rows(lo, hi):
        if lo < G_HALO:
            hb_ref[lo:G_HALO, :] = jnp.where(first, 0.0, _rmsnorm(xp_ref[lo:, :], g)).astype(BF16)
        m_lo, m_hi = max(lo, G_HALO), min(hi, G_HALO + tm)
        hb_ref[m_lo:m_hi, :] = _rmsnorm(x_ref[m_lo - G_HALO:m_hi - G_HALO, :], g).astype(BF16)
        if hi > G_HALO + tm:
            hb_ref[G_HALO + tm:hi, :] = jnp.where(last, 0.0, _rmsnorm(xn_ref[:hi - G_HALO - tm, :], g)).astype(BF16)

    def glu_rows(lo, hi):
        zc = jnp.dot(hb_ref[lo:hi, :], wc_ref[...], preferred_element_type=F32)
        glu = zc[:, :CONV_CH] * _sigmoid(zc[:, CONV_CH:])
        for c in range(N_SLAB):
            gl_ref[c, lo:hi, :] = glu[:, c * LANES:(c + 1) * LANES]

    def qkv_rows(r):
        z = jnp.dot(hb_ref[G_HALO + r:G_HALO + r + SB, :], wqkv_ref[...], preferred_element_type=F32)
        q_ref[r:r + SB, :] = (z[:, :ATTN_WIDTH] * (SCALE * LOG2E)).astype(BF16)
        kf = z[:, ATTN_WIDTH:ATTN_WIDTH + KV_WIDTH]
        vf = z[:, ATTN_WIDTH + KV_WIDTH:]
        k_ref[r:r + SB, :KV_WIDTH] = kf.astype(BF16)
        k_ref[r:r + SB, KV_WIDTH:] = pltpu.roll(kf, HEAD_DIM, axis=1).astype(BF16)
        v_ref[r:r + SB, :KV_WIDTH] = vf.astype(BF16)
        v_ref[r:r + SB, KV_WIDTH:] = pltpu.roll(vf, HEAD_DIM, axis=1).astype(BF16)

    def conv_rows(r):
        for c in range(N_SLAB):
            lanes = slice(c * LANES, (c + 1) * LANES)
            for rr in range(r, r + SB, CONV_ROWS):
                acc = jnp.broadcast_to(cb_ref[:, lanes], (CONV_ROWS, LANES))
                for t in range(CONV_K):
                    lo = rr + (G_HALO - CONV_PAD) + t
                    acc = acc + gl_ref[c, lo:lo + CONV_ROWS, :] * cw_ref[t:t + 1, lanes]
                cv_ref[rr:rr + CONV_ROWS, lanes] = acc
        cv = cv_ref[r:r + SB, :]
        mu = jnp.mean(cv, axis=-1, keepdims=True)
        xc = cv - mu
        ln = (xc * lax.rsqrt(jnp.mean(xc * xc, axis=-1, keepdims=True) + EPS)) * lng_ref[...] + lnb_ref[...]
        c_ref[r:r + SB, :] = (ln * _sigmoid(ln)).astype(BF16)

    n_sb = tm // SB
    bounds = [0] + [(k + 1) * SB + 2 * G_HALO for k in range(n_sb)]
    norm_rows(bounds[0], bounds[1])
    glu_rows(bounds[0], bounds[1])
    for sb in range(n_sb):
        if sb + 1 < n_sb:
            norm_rows(bounds[sb + 1], bounds[sb + 2])
            glu_rows(bounds[sb + 1], bounds[sb + 2])
        qkv_rows(sb * SB)
        conv_rows(sb * SB)


def _inproj(x, g, wqkv, wc, cw, cb, lng, lnb):
    b, s, _ = x.shape
    tm = min(TM_IN, s // IN_MIN_TILES)
    return pl.pallas_call(
        functools.partial(_inproj_kernel, tm=tm),
        grid=(b, s // tm),
        in_specs=_halo_specs(tm, G_HALO, s) + [
            _resident((1, D_MODEL)),
            _resident((D_MODEL, QKV_WIDTH)),
            _resident((D_MODEL, 2 * CONV_CH)),
            _resident((CONV_K, CONV_CH)),
            _resident((1, CONV_CH)),
            _resident((1, CONV_CH)),
            _resident((1, CONV_CH)),
        ],
        out_specs=[
            pl.BlockSpec((None, tm, ATTN_WIDTH), lambda bi, i: (bi, i, 0)),
            pl.BlockSpec((None, tm, 2 * KV_WIDTH), lambda bi, i: (bi, i, 0)),
            pl.BlockSpec((None, tm, 2 * KV_WIDTH), lambda bi, i: (bi, i, 0)),
            pl.BlockSpec((None, tm, CONV_CH), lambda bi, i: (bi, i, 0)),
        ],
        out_shape=[
            jax.ShapeDtypeStruct((b, s, ATTN_WIDTH), BF16),
            jax.ShapeDtypeStruct((b, s, 2 * KV_WIDTH), BF16),
            jax.ShapeDtypeStruct((b, s, 2 * KV_WIDTH), BF16),
            jax.ShapeDtypeStruct((b, s, CONV_CH), BF16),
        ],
        scratch_shapes=[
            pltpu.VMEM((tm + 2 * G_HALO, D_MODEL), BF16),
            pltpu.VMEM((N_SLAB, tm + 2 * G_HALO, LANES), F32),
            pltpu.VMEM((tm, CONV_CH), F32),
        ],
        compiler_params=pltpu.CompilerParams(
            dimension_semantics=("arbitrary", "arbitrary"), vmem_limit_bytes=VMEM_LIMIT_BYTES),
        name="inproj",
    )(x, x, x, g, wqkv, wc, cw, cb, lng, lnb)


def _mixer_kernel(sink_ref, x_ref, q_ref, k_ref, v_ref, c_ref, bias_ref, woa_ref, woc_ref, out_ref,
                  kz_ref, vz_ref, y_ref, sa_ref, sb_ref, *, seq):
    i = pl.program_id(1)
    r0 = pl.multiple_of(i * TQ, TQ)
    n_blk = TQ // BLOCK

    left = pl.multiple_of(jnp.maximum(r0 - BLOCK, 0), BLOCK)
    right = pl.multiple_of(jnp.minimum(r0 + TQ, seq - BLOCK), BLOCK)
    parts = ((0, left, BLOCK), (BLOCK, r0, TQ), (BLOCK + TQ, right, BLOCK))
    lo = lax.broadcasted_iota(jnp.int32, (1, LANES), 1) < HEAD_DIM
    for src_ref, dst_ref, fill in ((k_ref, kz_ref, jnp.zeros((), BF16)), (v_ref, vz_ref, jnp.ones((), BF16))):
        for dst0, src0, rows in parts:
            w = src_ref[pl.ds(src0, rows), :]
            straight, swapped = w[:, :KV_WIDTH], w[:, KV_WIDTH:]
            dst = slice(dst0, dst0 + rows)
            dst_ref[0, dst, :] = jnp.where(lo, straight, fill)
            dst_ref[1, dst, :] = jnp.where(lo, fill, swapped)
            dst_ref[2, dst, :] = jnp.where(lo, swapped, fill)
            dst_ref[3, dst, :] = jnp.where(lo, fill, straight)

    dyn0 = pl.multiple_of(jnp.minimum(i, 0) * BLOCK, BLOCK)

    def block_pos(jb):
        r = r0 + jb * BLOCK
        var = jnp.where(r == 0, 1, jnp.where(r == seq - BLOCK, 2, 0))
        return jb * BLOCK, var

    def score_stage(jb, s_ref):
        koff, _ = block_pos(jb)
        row = jb * BLOCK
        for n in range(N_HEADS):
            m, e = divmod(n, 2)
            h = m // (GROUP // 2)
            qp = q_ref[row:row + BLOCK, m * LANES:(m + 1) * LANES]
            kz = kz_ref[2 * h + e, pl.ds(koff, BAND), :]
            s_ref[n] = lax.dot_general(qp, kz, (((1,), (1,)), ((), ())), preferred_element_type=F32)

    def value_stage(jb, s_ref):
        koff, var = block_pos(jb)
        row = jb * BLOCK
        for m in range(N_HEADS // 2):
            h = m // (GROUP // 2)
            pv = []
            sk = []
            for e in range(2):
                n = 2 * m + e
                s = s_ref[n, pl.ds(dyn0, BLOCK), :] + bias_ref[var, n]
                sink = sink_ref[n] * LOG2E
                mx = jnp.maximum(jnp.max(s, axis=-1, keepdims=True), sink)
                p = jnp.exp2(s - mx)
                sk.append(jnp.broadcast_to(jnp.exp2(sink - mx), (BLOCK, LANES)))
                vz = vz_ref[2 * h + e, pl.ds(koff, BAND), :]
                pv.append(jnp.dot(p.astype(BF16), vz, preferred_element_type=F32))
            outs = jnp.where(lo, pv[0], pv[1])
            sums = pltpu.roll(jnp.where(lo, pv[1], pv[0]), HEAD_DIM, axis=1)
            denom = sums + jnp.where(lo, sk[0], sk[1])
            y_ref[row:row + BLOCK, m * LANES:(m + 1) * LANES] = (outs / denom).astype(BF16)

    def out_proj(jb, n=1):
        rows = slice(jb * BLOCK, (jb + n) * BLOCK)
        out_ref[rows, :] = (x_ref[rows, :]
                            + jnp.dot(y_ref[rows, :], woa_ref[...], preferred_element_type=F32)
                            + jnp.dot(c_ref[rows, :], woc_ref[...], preferred_element_type=F32))

    stage = (sa_ref, sb_ref)
    score_stage(0, stage[0])
    for jb in range(n_blk):
        if jb + 1 < n_blk:
            score_stage(jb + 1, stage[(jb + 1) % 2])
        if jb > 0 and jb % 2 == 0:
            out_proj(jb - 2, 2)
        value_stage(jb, stage[jb % 2])
    out_proj(n_blk - 2, 2)


def _mixer(x, q, k, v, c_act, tables, sink, woa, woc):
    b, s, _ = x.shape
    kernel = functools.partial(_mixer_kernel, seq=s)
    return pl.pallas_call(
        kernel,
        grid=(b, s // TQ),
        in_specs=[
            pl.BlockSpec(memory_space=pltpu.SMEM),
            pl.BlockSpec((None, TQ, D_MODEL), lambda bi, i: (bi, i, 0)),
            pl.BlockSpec((None, TQ, ATTN_WIDTH), lambda bi, i: (bi, i, 0)),
            pl.BlockSpec((None, s, 2 * KV_WIDTH), lambda bi, i: (bi, 0, 0)),
            pl.BlockSpec((None, s, 2 * KV_WIDTH), lambda bi, i: (bi, 0, 0)),
            pl.BlockSpec((None, TQ, CONV_CH), lambda bi, i: (bi, i, 0)),
            _resident((N_VARIANT, N_HEADS, BLOCK, BAND)),
            _resident((ATTN_WIDTH, D_MODEL)),
            _resident((CONV_CH, D_MODEL)),
        ],
        out_specs=pl.BlockSpec((None, TQ, D_MODEL), lambda bi, i: (bi, i, 0)),
        out_shape=jax.ShapeDtypeStruct((b, s, D_MODEL), F32),
        scratch_shapes=[
            pltpu.VMEM((2 * N_KV, TQ + 2 * BLOCK, LANES), BF16),
            pltpu.VMEM((2 * N_KV, TQ + 2 * BLOCK, LANES), BF16),
            pltpu.VMEM((TQ, ATTN_WIDTH), BF16),
            pltpu.VMEM((N_HEADS, BLOCK, BAND), F32),
            pltpu.VMEM((N_HEADS, BLOCK, BAND), F32),
        ],
        compiler_params=pltpu.CompilerParams(
            dimension_semantics=("arbitrary", "arbitrary"), vmem_limit_bytes=VMEM_LIMIT_BYTES),
        name="mixer",
    )(sink, x, q, k, v, c_act, tables, woa, woc)


def _ffn_kernel(x_ref, xp_ref, xn_ref, g_ref, wup_ref, dw_ref, db_ref, wdn_ref, gf_ref, out_ref,
                hb_ref, uga_ref, uva_ref, ugb_ref, uvb_ref, *, final_norm):
    i = pl.program_id(1)
    first = i == 0
    last = i == pl.num_programs(1) - 1
    tm = TM_FF
    g = g_ref[...]
    hb_ref[:U_HALO, :] = jnp.where(first, 0.0, _rmsnorm(xp_ref[...], g)).astype(BF16)
    hb_ref[U_HALO:U_HALO + tm, :] = _rmsnorm(x_ref[...], g).astype(BF16)
    hb_ref[U_HALO + tm:, :] = jnp.where(last, 0.0, _rmsnorm(xn_ref[...], g)).astype(BF16)
    out_ref[...] = x_ref[...]

    def up_proj(j, ug_ref, uv_ref):
        hb = hb_ref[...]
        ug = jnp.dot(hb, wup_ref[0, j], preferred_element_type=F32)
        uv = jnp.dot(hb, wup_ref[1, j], preferred_element_type=F32)
        for c in range(CJ // LANES):
            ug_ref[c] = ug[:, c * LANES:(c + 1) * LANES]
            uv_ref[c] = uv[:, c * LANES:(c + 1) * LANES]

    def conv3(u_ref, c, w, bias):
        lanes = slice(c * LANES, (c + 1) * LANES)
        out = bias[:, lanes] + u_ref[c, U_HALO - 1:U_HALO - 1 + tm, :] * w[0:1, lanes]
        out = out + u_ref[c, U_HALO:U_HALO + tm, :] * w[1:2, lanes]
        return out + u_ref[c, U_HALO + 1:U_HALO + 1 + tm, :] * w[2:3, lanes]

    def down_proj(j, ug_ref, uv_ref):
        acts = []
        for c in range(CJ // LANES):
            gate = conv3(ug_ref, c, dw_ref[0, j], db_ref[0, j])
            val = conv3(uv_ref, c, dw_ref[1, j], db_ref[1, j])
            acts.append(((gate * _sigmoid(gate)) * val).astype(BF16))
        act = jnp.concatenate(acts, axis=1)
        out_ref[...] += jnp.dot(act, wdn_ref[j], preferred_element_type=F32)

    up_proj(0, uga_ref, uva_ref)

    def chunk_pair(j):
        up_proj(j + 1, ugb_ref, uvb_ref)
        down_proj(j, uga_ref, uva_ref)
        up_proj(j + 2, uga_ref, uva_ref)
        down_proj(j + 1, ugb_ref, uvb_ref)

    def chunk_quad(p, carry):
        chunk_pair(4 * p)
        chunk_pair(4 * p + 2)
        return carry

    n_quad = (NJ - 1) // 4
    lax.fori_loop(0, n_quad, chunk_quad, 0)
    for j in range(4 * n_quad, NJ - 1, 2):
        chunk_pair(j)
    down_proj(NJ - 1, uga_ref, uva_ref)
    if final_norm:
        out_ref[...] = _rmsnorm(out_ref[...], gf_ref[...])


def _ffn(x, g, wup, dw, db, wdn, gf, final_norm):
    b, s, _ = x.shape
    tm = TM_FF
    kernel = functools.partial(_ffn_kernel, final_norm=final_norm)
    return pl.pallas_call(
        kernel,
        grid=(b, s // tm),
        in_specs=_halo_specs(tm, U_HALO, s) + [
            _resident((1, D_MODEL)),
            _resident((2, NJ, D_MODEL, CJ)),
            _resident((2, NJ, FFN_CONV_K, CJ)),
            _resident((2, NJ, 1, CJ)),
            _resident((NJ, CJ, D_MODEL)),
            _resident((1, D_MODEL)),
        ],
        out_specs=pl.BlockSpec((None, tm, D_MODEL), lambda bi, i: (bi, i, 0)),
        out_shape=jax.ShapeDtypeStruct((b, s, D_MODEL), F32),
        scratch_shapes=[pltpu.VMEM((tm + 2 * U_HALO, D_MODEL), BF16)]
        + [pltpu.VMEM((CJ // LANES, tm + 2 * U_HALO, LANES), F32)] * 4,
        compiler_params=pltpu.CompilerParams(
            dimension_semantics=("arbitrary", "arbitrary"), vmem_limit_bytes=VMEM_LIMIT_BYTES),
        name="ffn",
    )(x, x, x, g, wup, dw, db, wdn, gf)


def _split_ff(a):
    lead = a.shape[:-1]
    a = a.reshape(lead + (2, NJ, CJ))
    return jnp.moveaxis(a, (-3, -2), (0, 1))


def _prepare_layer(l, norm_attn_g, w_in, attn_sink, conv_dw_w, conv_dw_b, conv_ln_g, conv_ln_b, w_out,
                   norm_ffn_g, w_up, ffn_dw_w, ffn_dw_b, w_down):
    return dict(
        g_attn=norm_attn_g[l].astype(F32)[None, :],
        wqkv=w_in[l][:, :QKV_WIDTH].astype(BF16),
        wc=w_in[l][:, QKV_WIDTH:].astype(BF16),
        sink=attn_sink[l].astype(F32),
        cw=conv_dw_w[l].astype(F32),
        cb=conv_dw_b[l].astype(F32)[None, :],
        lng=conv_ln_g[l].astype(F32)[None, :],
        lnb=conv_ln_b[l].astype(F32)[None, :],
        woa=w_out[l][:ATTN_WIDTH].astype(BF16),
        woc=w_out[l][ATTN_WIDTH:].astype(BF16),
        g_ffn=norm_ffn_g[l].astype(F32)[None, :],
        wup=_split_ff(w_up[l].astype(BF16)),
        dw=_split_ff(ffn_dw_w[l].astype(F32)),
        db=_split_ff(ffn_dw_b[l].astype(F32)[None, :]),
        wdn=w_down[l].astype(BF16).reshape(NJ, CJ, D_MODEL),
    )


def _trunk(x, tables, layers, gf):
    depth = len(layers)
    for l, p in enumerate(layers):
        q, k, v, c_act = _inproj(x, p["g_attn"], p["wqkv"], p["wc"], p["cw"], p["cb"], p["lng"], p["lnb"])
        x = _mixer(x, q, k, v, c_act, tables, p["sink"], p["woa"], p["woc"])
        x = _ffn(x, p["g_ffn"], p["wup"], p["dw"], p["db"], p["wdn"], gf, final_norm=(l == depth - 1))
    return x


def kernel(x_prompt, x_sample, rel_bias, norm_attn_g, w_in, attn_sink, conv_dw_w, conv_dw_b, conv_ln_g,
           conv_ln_b, w_out, norm_ffn_g, w_up, ffn_dw_w, ffn_dw_b, w_down, norm_final_g):
    depth = w_in.shape[0]
    tables = _band_tables(rel_bias)
    layers = [
        _prepare_layer(l, norm_attn_g, w_in, attn_sink, conv_dw_w, conv_dw_b, conv_ln_g, conv_ln_b, w_out,
                       norm_ffn_g, w_up, ffn_dw_w, ffn_dw_b, w_down)
        for l in range(depth)
    ]
    gf = norm_final_g.astype(F32)[None, :]
    return (_trunk(x_prompt, tables, layers, gf), _trunk(x_sample, tables, layers, gf))
```
